```python
import math
import jax
import jax.numpy as jnp
from jax import lax
import numpy as np

D_MODEL = 2048
BATCH = 2
SEQ = 16384
DEPTH = 2

F32 = jnp.float32
CHUNK = 64
HEAD_DIM = 128
N_MIX_HEADS = D_MODEL // HEAD_DIM
SGU_GROUPS = N_MIX_HEADS // 4
SGU_WIDTH = SGU_GROUPS * HEAD_DIM
SGU_BLOCK = 128
DN_HEADS = (N_MIX_HEADS - SGU_GROUPS) // 2
DN_DK = HEAD_DIM
DN_DV = HEAD_DIM
DN_CONV = 4
DIFF_HEADS = N_MIX_HEADS - SGU_GROUPS - DN_HEADS
DIFF_QK_DIM = HEAD_DIM // 2
DIFF_V_DIM = HEAD_DIM
ROPE_THETA = 10000.0
Q_BLOCK = 128
MIX_WIDTH = SGU_WIDTH + DN_HEADS * DN_DV + DIFF_HEADS * DIFF_V_DIM
SPLITS = [SGU_WIDTH, SGU_WIDTH,
          DN_HEADS * DN_DK, DN_HEADS * DN_DK, DN_HEADS * DN_DV, DN_HEADS * DN_DV, DN_HEADS, DN_HEADS,
          DIFF_HEADS * 2 * DIFF_QK_DIM, DIFF_HEADS * 2 * DIFF_QK_DIM, DIFF_HEADS * DIFF_V_DIM]
IN_WIDTH = sum(SPLITS)
SPLIT_POINTS = [sum(SPLITS[:i + 1]) for i in range(len(SPLITS) - 1)]
N_EXPERTS = 32
N_EXPERT_GROUPS = 4
TOP_K = 2
D_EXPERT = D_MODEL // 2
MOE_BLOCK = 256
DEEPNORM_ALPHA = (2 * DEPTH) ** 0.25
DEEPNORM_BETA = (8 * DEPTH) ** -0.25
LN_EPS = 1e-5
RMS_EPS = 1e-6

kernel_name = "hybrid_streaming_encoder_block"


def layer_norm(x, g, b):
    xf = x.astype(F32)
    mu = xf.mean(-1, keepdims=True)
    var = jnp.square(xf - mu).mean(-1, keepdims=True)
    return ((xf - mu) * lax.rsqrt(var + LN_EPS) * g.astype(F32) + b.astype(F32)).astype(x.dtype)


def rms_norm(x, g):
    xf = x.astype(F32)
    y = xf * lax.rsqrt(jnp.mean(xf * xf, -1, keepdims=True) + RMS_EPS) * g.astype(F32)
    return y.astype(x.dtype)


def l2_normalize(x):
    xf = x.astype(F32)
    return xf * lax.rsqrt(jnp.sum(xf * xf, -1, keepdims=True) + RMS_EPS)


def causal_depthwise_conv(x, w):
    K, C = w.shape
    return lax.conv_general_dilated(
        x, w[:, None, :].astype(x.dtype), window_strides=(1,), padding=[(K - 1, 0)],
        dimension_numbers=('NWC', 'WIO', 'NWC'), feature_group_count=C)


def apply_rotary(x):
    S, d = x.shape[1], x.shape[-1]
    half = d // 2
    inv_freq = ROPE_THETA ** (-jnp.arange(half, dtype=F32) / half)
    ang = jnp.arange(S, dtype=F32)[:, None] * inv_freq[None, :]
    cos = jnp.cos(ang)[None, :, None, None, :]
    sin = jnp.sin(ang)[None, :, None, None, :]
    x1 = x[..., :half].astype(F32)
    x2 = x[..., half:].astype(F32)
    return jnp.concatenate([x1 * cos - x2 * sin, x1 * sin + x2 * cos], -1).astype(x.dtype)


def spatial_gating(u, v, norm_g, norm_b, w_s, b_s):
    B, S, _ = u.shape
    nb = S // SGU_BLOCK
    u = jax.nn.gelu(u, approximate=False).reshape(B, nb, SGU_BLOCK, SGU_GROUPS, HEAD_DIM)
    v = jax.nn.gelu(v, approximate=False).reshape(B, nb, SGU_BLOCK, SGU_GROUPS, HEAD_DIM)
    v = layer_norm(v, norm_g.reshape(SGU_GROUPS, HEAD_DIM), norm_b.reshape(SGU_GROUPS, HEAD_DIM))
    pos_chunk = jnp.arange(SGU_BLOCK) // CHUNK
    allowed = pos_chunk[None, :] <= pos_chunk[:, None]
    w = jnp.where(allowed[None], w_s, 0.0).astype(v.dtype)
    s = jnp.einsum('gij,bnjgc->bnigc', w, v) + b_s.T[:, :, None].astype(v.dtype)
    return (u * s).reshape(B, S, SGU_WIDTH)


def chunk_gated_delta_rule(q, k, v, g, beta):
    B, S, H, DK = q.shape
    DV = v.shape[-1]
    N = S // CHUNK

    def to_chunks(t):
        t = t.reshape((B, N, CHUNK, H) + t.shape[3:])
        return jnp.moveaxis(t, 3, 1)

    q, k, v = to_chunks(q), to_chunks(k), to_chunks(v)
    g = jnp.cumsum(to_chunks(g), axis=-1)
    beta = to_chunks(beta)
    idx = jnp.arange(CHUNK)
    causal = idx[:, None] >= idx[None, :]
    strict = idx[:, None] > idx[None, :]
    decay = jnp.exp(jnp.where(causal, g[..., :, None] - g[..., None, :], -jnp.inf))
    k_beta = k * beta[..., None]
    m = jnp.where(strict, jnp.einsum('bhnid,bhnjd->bhnij', k_beta, k) * decay, 0.0)
    eye = jnp.eye(CHUNK, dtype=F32)
    rhs = jnp.concatenate([v * beta[..., None], k_beta * jnp.exp(g)[..., None]], -1)
    sol = lax.linalg.triangular_solve(m + eye, rhs, left_side=True, lower=True, unit_diagonal=True)
    u, w = sol[..., :DV], sol[..., DV:]
    attn = jnp.einsum('bhnid,bhnjd->bhnij', q, k) * decay
    xs = tuple(jnp.moveaxis(t, 2, 0) for t in (q, k, u, w, g, attn))

    def step(state, inp):
        q_c, k_c, u_c, w_c, g_c, attn_c = inp
        v_new = u_c - jnp.einsum('bhck,bhkv->bhcv', w_c, state)
        o_c = (jnp.einsum('bhck,bhkv->bhcv', q_c * jnp.exp(g_c)[..., None], state)
               + jnp.einsum('bhij,bhjv->bhiv', attn_c, v_new))
        g_last = g_c[..., -1]
        k_dec = k_c * jnp.exp(g_last[..., None] - g_c)[..., None]
        state = state * jnp.exp(g_last)[..., None, None] + jnp.einsum('bhck,bhcv->bhkv', k_dec, v_new)
        return state, o_c

    state0 = jnp.zeros((B, H, DK, DV), F32)
    _, o = lax.scan(step, state0, xs)
    o = jnp.moveaxis(o, 0, 2)
    return jnp.moveaxis(o, 1, 3).reshape(B, S, H, DV)


def gated_deltanet(q, k, v, gate, a, b, conv_w, a_log, dt_bias, norm_g):
    B, S, _ = q.shape
    qkv = jax.nn.silu(causal_depthwise_conv(jnp.concatenate([q, k, v], -1), conv_w))
    q, k, v = jnp.split(qkv, [DN_HEADS * DN_DK, 2 * DN_HEADS * DN_DK], axis=-1)
    q = l2_normalize(q.reshape(B, S, DN_HEADS, DN_DK)) * DN_DK ** -0.5
    k = l2_normalize(k.reshape(B, S, DN_HEADS, DN_DK))
    v = v.reshape(B, S, DN_HEADS, DN_DV).astype(F32)
    beta = jax.nn.sigmoid(b.astype(F32))
    g = -jnp.exp(a_log.astype(F32)) * jax.nn.softplus(a.astype(F32) + dt_bias.astype(F32))
    o = chunk_gated_delta_rule(q, k, v, g, beta)
    o = rms_norm(o, norm_g) * jax.nn.silu(gate.reshape(B, S, DN_HEADS, DN_DV).astype(F32))
    return o.reshape(B, S, DN_HEADS * DN_DV).astype(gate.dtype)


def diff_attention(q, k, v, lam_q1, lam_k1, lam_q2, lam_k2, norm_g, lambda_init):
    B, S, _ = q.shape
    q = apply_rotary(q.reshape(B, S, DIFF_HEADS, 2, DIFF_QK_DIM))
    k = apply_rotary(k.reshape(B, S, DIFF_HEADS, 2, DIFF_QK_DIM))
    v = v.reshape(B, S, DIFF_HEADS, DIFF_V_DIM)
    lam = (jnp.exp(jnp.sum(lam_q1.astype(F32) * lam_k1.astype(F32)))
           - jnp.exp(jnp.sum(lam_q2.astype(F32) * lam_k2.astype(F32))) + lambda_init)
    nq = S // Q_BLOCK
    qb = jnp.moveaxis(q.reshape(B, nq, Q_BLOCK, DIFF_HEADS, 2, DIFF_QK_DIM), 1, 0)
    k_chunk = jnp.arange(S) // CHUNK
    scale = DIFF_QK_DIM ** -0.5

    def attend(args):
        q_blk, blk = args
        s = jnp.einsum('bqhtd,bkhtd->bhtqk', q_blk, k).astype(F32) * scale
        q_chunk = (blk * Q_BLOCK + jnp.arange(Q_BLOCK)) // CHUNK
        s = jnp.where(k_chunk[None, :] <= q_chunk[:, None], s, -jnp.inf)
        p = jax.nn.softmax(s, axis=-1)
        p = p[:, :, 0] - lam * p[:, :, 1]
        return jnp.einsum('bhqk,bkhd->bqhd', p.astype(v.dtype), v)

    o = lax.map(attend, (qb, jnp.arange(nq)))
    o = jnp.moveaxis(o, 0, 1).reshape(B, S, DIFF_HEADS, DIFF_V_DIM)
    o = rms_norm(o, norm_g) * (1.0 - lambda_init)
    return o.reshape(B, S, DIFF_HEADS * DIFF_V_DIM)


def grouped_moe(x, router_w, router_bias, w_gate, w_up, w_down):
    B, S, D = x.shape
    T = B * S
    A = T * TOP_K
    eg = N_EXPERTS // N_EXPERT_GROUPS
    xf = x.reshape(T, D)
    scores = jax.nn.sigmoid(jnp.dot(xf, router_w).astype(F32))
    sel = (scores + router_bias.astype(F32)).reshape(T, N_EXPERT_GROUPS, eg)
    group_score = lax.top_k(sel, TOP_K)[0].sum(-1)
    best = jnp.argmax(group_score, axis=-1).astype(jnp.int32)
    in_group = jnp.take_along_axis(sel, best[:, None, None], axis=1)[:, 0]
    _, local = lax.top_k(in_group, TOP_K)
    expert_idx = best[:, None] * eg + local.astype(jnp.int32)
    gate = jnp.take_along_axis(scores, expert_idx, axis=1)
    gate = gate / gate.sum(-1, keepdims=True)
    flat_e = expert_idx.reshape(A)
    flat_tok = jnp.repeat(jnp.arange(T, dtype=jnp.int32), TOP_K)
    order = jnp.argsort(flat_e)
    e_sorted = flat_e[order]
    tok_sorted = flat_tok[order]
    gate_sorted = gate.reshape(A)[order]
    counts = jnp.bincount(flat_e, length=N_EXPERTS)
    padded = (counts + MOE_BLOCK - 1) // MOE_BLOCK * MOE_BLOCK
    pad_end = jnp.cumsum(padded)
    pad_start = pad_end - padded
    start = jnp.cumsum(counts) - counts
    dest = pad_start[e_sorted] + jnp.arange(A, dtype=jnp.int32) - start[e_sorted]
    n_blocks = -(-A // MOE_BLOCK) + N_EXPERTS
    P = n_blocks * MOE_BLOCK
    buf_tok = jnp.full((P,), T, jnp.int32).at[dest].set(tok_sorted)
    buf_gate = jnp.zeros((P,), F32).at[dest].set(gate_sorted)
    block_expert = jnp.minimum(
        jnp.searchsorted(pad_end, jnp.arange(n_blocks, dtype=jnp.int32) * MOE_BLOCK, side='right'),
        N_EXPERTS - 1)
    x_pad = jnp.concatenate([xf, jnp.zeros((1, D), xf.dtype)], 0)

    def expert_block(args):
        tok, e = args
        xb = x_pad[tok]
        hb = jax.nn.silu(xb @ w_gate[e]) * (xb @ w_up[e])
        return hb @ w_down[e]

    yb = lax.map(expert_block, (buf_tok.reshape(n_blocks, MOE_BLOCK), block_expert))
    yb = yb.reshape(P, D) * buf_gate[:, None].astype(x.dtype)
    out = jnp.zeros((T + 1, D), x.dtype).at[buf_tok].add(yb)[:T]
    return out.reshape(B, S, D)


def setup_inputs(seed: int = 0) -> dict:
    key = jax.random.key(seed)
    ks = jax.random.split(key, 25)

    def nrm(k, shape, scale):
        return jax.random.normal(k, shape, F32) * scale

    x = nrm(ks[0], (BATCH, SEQ, D_MODEL), 1.0)
    w_in = nrm(ks[1], (DEPTH, D_MODEL, IN_WIDTH), D_MODEL ** -0.5)
    sgu_norm_g = 1.0 + nrm(ks[2], (DEPTH, SGU_WIDTH), 0.1)
    sgu_norm_b = nrm(ks[3], (DEPTH, SGU_WIDTH), 0.02)
    sgu_w = nrm(ks[4], (DEPTH, SGU_GROUPS, SGU_BLOCK, SGU_BLOCK), 0.5 * SGU_BLOCK ** -0.5)
    sgu_b = 1.0 + nrm(ks[5], (DEPTH, SGU_GROUPS, SGU_BLOCK), 0.1)
    dn_conv_w = nrm(ks[6], (DEPTH, DN_CONV, 2 * DN_HEADS * DN_DK + DN_HEADS * DN_DV), DN_CONV ** -0.5)
    dn_a_log = jnp.log(jax.random.uniform(ks[7], (DEPTH, DN_HEADS), F32, 1.0, 16.0))
    dt = jnp.exp(jax.random.uniform(ks[8], (DEPTH, DN_HEADS), F32, math.log(1e-3), math.log(1e-1)))
    dn_dt_bias = dt + jnp.log(-jnp.expm1(-dt))
    dn_norm_g = 1.0 + nrm(ks[9], (DEPTH, DN_DV), 0.1)
    diff_lambda_q1 = nrm(ks[10], (DEPTH, DIFF_QK_DIM), 0.1)
    diff_lambda_k1 = nrm(ks[11], (DEPTH, DIFF_QK_DIM), 0.1)
    diff_lambda_q2 = nrm(ks[12], (DEPTH, DIFF_QK_DIM), 0.1)
    diff_lambda_k2 = nrm(ks[13], (DEPTH, DIFF_QK_DIM), 0.1)
    diff_norm_g = 1.0 + nrm(ks[14], (DEPTH, DIFF_V_DIM), 0.1)
    w_out = nrm(ks[15], (DEPTH, MIX_WIDTH, D_MODEL), MIX_WIDTH ** -0.5 * DEEPNORM_BETA)
    ln1_g = 1.0 + nrm(ks[16], (DEPTH, D_MODEL), 0.1)
    ln1_b = nrm(ks[17], (DEPTH, D_MODEL), 0.02)
    router_w = nrm(ks[18], (D_MODEL, N_EXPERTS), D_MODEL ** -0.5)
    router_bias = nrm(ks[19], (N_EXPERTS,), 0.01)
    moe_w_gate = nrm(ks[20], (DEPTH, N_EXPERTS, D_MODEL, D_EXPERT), D_MODEL ** -0.5)
    moe_w_up = nrm(ks[21], (DEPTH, N_EXPERTS, D_MODEL, D_EXPERT), D_MODEL ** -0.5)
    moe_w_down = nrm(ks[22], (DEPTH, N_EXPERTS, D_EXPERT, D_MODEL), D_EXPERT ** -0.5 * DEEPNORM_BETA)
    ln2_g = 1.0 + nrm(ks[23], (DEPTH, D_MODEL), 0.1)
    ln2_b = nrm(ks[24], (DEPTH, D_MODEL), 0.02)
    return {"x": x, "w_in": w_in, "sgu_norm_g": sgu_norm_g, "sgu_norm_b": sgu_norm_b,
            "sgu_w": sgu_w, "sgu_b": sgu_b, "dn_conv_w": dn_conv_w, "dn_a_log": dn_a_log,
            "dn_dt_bias": dn_dt_bias, "dn_norm_g": dn_norm_g, "diff_lambda_q1": diff_lambda_q1,
            "diff_lambda_k1": diff_lambda_k1, "diff_lambda_q2": diff_lambda_q2,
            "diff_lambda_k2": diff_lambda_k2, "diff_norm_g": diff_norm_g, "w_out": w_out,
            "ln1_g": ln1_g, "ln1_b": ln1_b, "router_w": router_w, "router_bias": router_bias,
            "moe_w_gate": moe_w_gate, "moe_w_up": moe_w_up, "moe_w_down": moe_w_down,
            "ln2_g": ln2_g, "ln2_b": ln2_b}


def reference(x, w_in, sgu_norm_g, sgu_norm_b, sgu_w, sgu_b, dn_conv_w, dn_a_log, dn_dt_bias,
              dn_norm_g, diff_lambda_q1, diff_lambda_k1, diff_lambda_q2, diff_lambda_k2,
              diff_norm_g, w_out, ln1_g, ln1_b, router_w, router_bias, moe_w_gate, moe_w_up,
              moe_w_down, ln2_g, ln2_b):
    for l in range(DEPTH):
        h = jnp.einsum('bsd,de->bse', x, w_in[l])
        (a_u, a_v, b_q, b_k, b_v, b_gate, b_a, b_b, c_q, c_k, c_v) = jnp.split(h, SPLIT_POINTS, axis=-1)
        y_a = spatial_gating(a_u, a_v, sgu_norm_g[l], sgu_norm_b[l], sgu_w[l], sgu_b[l])
        y_b = gated_deltanet(b_q, b_k, b_v, b_gate, b_a, b_b, dn_conv_w[l], dn_a_log[l],
                             dn_dt_bias[l], dn_norm_g[l])
        lambda_init = 0.8 - 0.6 * math.exp(-0.3 * l)
        y_c = diff_attention(c_q, c_k, c_v, diff_lambda_q1[l], diff_lambda_k1[l], diff_lambda_q2[l],
                             diff_lambda_k2[l], diff_norm_g[l], lambda_init)
        mixed = jnp.einsum('bsm,md->bsd', jnp.concatenate([y_a, y_b, y_c], -1), w_out[l])
        x = layer_norm(DEEPNORM_ALPHA * x + mixed, ln1_g[l], ln1_b[l])
        ffn = grouped_moe(x, router_w, router_bias, moe_w_gate[l], moe_w_up[l], moe_w_down[l])
        x = layer_norm(DEEPNORM_ALPHA * x + ffn, ln2_g[l], ln2_b[l])
    return x
```

```python
import functools
import math

import numpy as np
import jax
import jax.numpy as jnp
from jax import lax
from jax.experimental import pallas as pl
from jax.experimental.pallas import tpu as pltpu

F32 = jnp.float32
BF16 = jnp.bfloat16
HIGHEST = lax.Precision.HIGHEST

D_MODEL = 2048
DEPTH = 2
CHUNK = 64
HEAD_DIM = 128
SGU_GROUPS = 4
SGU_WIDTH = SGU_GROUPS * HEAD_DIM
SGU_BLOCK = 128
DN_HEADS = 6
DN_CONV = 4
DIFF_HEADS = 6
DIFF_QK_DIM = 64
ROPE_THETA = 10000.0
N_EXPERTS = 32
N_EXPERT_GROUPS = 4
EXPERTS_PER_GROUP = N_EXPERTS // N_EXPERT_GROUPS
TOP_K = 2
D_EXPERT = D_MODEL // 2
DEEPNORM_ALPHA = (2 * DEPTH) ** 0.25
LN_EPS = 1e-5
RMS_EPS = 1e-6

LANES = 128
VMEM_LIMIT = 56 * 1024 * 1024

HW = DN_HEADS * HEAD_DIM
COL_BQ, COL_BK, COL_BV, COL_BG = 0, 6, 12, 18
COL_CQ, COL_CK, COL_CV = 24, 30, 36
COL_AU, COL_AV = 42, 46
COL_AB = 50
NW = 51 * LANES
N_TILES_IN = 3

MOE_BLK = 256


def _cparams(sem, vmem=VMEM_LIMIT):
    return pltpu.CompilerParams(dimension_semantics=sem, vmem_limit_bytes=vmem)


def _matmul_kernel(x_ref, w_ref, o_ref):
    o_ref[...] = jnp.dot(x_ref[...], w_ref[...], preferred_element_type=F32).astype(o_ref.dtype)


def in_proj(xb, w):
    T = xb.shape[0]
    tm = min(512, T)
    tn = NW // N_TILES_IN
    return pl.pallas_call(
        _matmul_kernel,
        grid=(N_TILES_IN, T // tm),
        in_specs=[pl.BlockSpec((tm, D_MODEL), lambda j, i: (i, 0)),
                  pl.BlockSpec((D_MODEL, tn), lambda j, i: (0, j))],
        out_specs=pl.BlockSpec((tm, tn), lambda j, i: (i, j)),
        out_shape=jax.ShapeDtypeStruct((T, NW), BF16),
        compiler_params=_cparams(("arbitrary", "arbitrary")),
        name="in_proj",
    )(xb, w)


def _gelu(t):
    return t * (lax.erf(t * (2.0 ** -0.5)) + 1.0) * 0.5


def _sgu_kernel(u_ref, v_ref, ng_ref, nb_ref, w_ref, b_ref, o_ref, *, rows):
    u = _gelu(u_ref[...].astype(F32))
    v = _gelu(v_ref[...].astype(F32))
    mu = jnp.mean(v, axis=-1, keepdims=True)
    var = jnp.mean(jnp.square(v - mu), axis=-1, keepdims=True)
    vn = ((v - mu) * lax.rsqrt(var + LN_EPS) * ng_ref[...] + nb_ref[...]).astype(BF16)
    ii = lax.broadcasted_iota(jnp.int32, (SGU_BLOCK, SGU_BLOCK), 0)
    jj = lax.broadcasted_iota(jnp.int32, (SGU_BLOCK, SGU_BLOCK), 1)
    w = jnp.where(jj // CHUNK <= ii // CHUNK, w_ref[...], 0.0).astype(BF16)
    bias = b_ref[...]
    for n in range(rows // SGU_BLOCK):
        sl = slice(n * SGU_BLOCK, (n + 1) * SGU_BLOCK)
        s = jnp.dot(w, vn[sl], preferred_element_type=F32) + bias
        o_ref[sl, :] = (u[sl] * s).astype(o_ref.dtype)


def sgu(h, norm_g, norm_b, w_s, b_s):
    T = h.shape[0]
    rows = min(512, T)
    return pl.pallas_call(
        functools.partial(_sgu_kernel, rows=rows),
        grid=(T // rows, SGU_GROUPS),
        in_specs=[pl.BlockSpec((rows, LANES), lambda i, g: (i, COL_AU + g)),
                  pl.BlockSpec((rows, LANES), lambda i, g: (i, COL_AV + g)),
                  pl.BlockSpec((1, LANES), lambda i, g: (0, g)),
                  pl.BlockSpec((1, LANES), lambda i, g: (0, g)),
                  pl.BlockSpec((None, SGU_BLOCK, SGU_BLOCK), lambda i, g: (g, 0, 0)),
                  pl.BlockSpec((None, SGU_BLOCK, 1), lambda i, g: (g, 0, 0))],
        out_specs=pl.BlockSpec((rows, LANES), lambda i, g: (i, g)),
        out_shape=jax.ShapeDtypeStruct((T, SGU_WIDTH), BF16),
        compiler_params=_cparams(("arbitrary", "arbitrary")),
        name="sgu",
    )(h, h, norm_g.reshape(1, SGU_WIDTH), norm_b.reshape(1, SGU_WIDTH), w_s,
      b_s.reshape(SGU_GROUPS, SGU_BLOCK, 1))


def _rope_kernel(x_ref, cos_ref, sin_ref, o_ref):
    x = x_ref[...].astype(F32)
    lane = lax.broadcasted_iota(jnp.int32, x.shape, 1)
    half = DIFF_QK_DIM // 2
    partner = jnp.where(lane % DIFF_QK_DIM < half,
                        pltpu.roll(x, LANES - half, axis=1), pltpu.roll(x, half, axis=1))
    scale = jnp.where(pl.program_id(1) < DIFF_HEADS, DIFF_QK_DIM ** -0.5, 1.0)
    o_ref[...] = ((x * cos_ref[...] + partner * sin_ref[...]) * scale).astype(o_ref.dtype)


def _rope_tables(S):
    half = DIFF_QK_DIM // 2
    inv_freq = ROPE_THETA ** (-jnp.arange(half, dtype=F32) / half)
    ang = jnp.arange(S, dtype=F32)[:, None] * inv_freq[None, :]
    cos, sin = jnp.cos(ang), jnp.sin(ang)
    cos_t = jnp.tile(cos, (1, LANES // half))
    sin_t = jnp.tile(jnp.concatenate([-sin, sin], axis=1), (1, LANES // DIFF_QK_DIM))
    return cos_t, sin_t


def rope(h, cos_t, sin_t):
    T = h.shape[0]
    S = cos_t.shape[0]
    rows = min(512, S)
    ns = S // rows
    return pl.pallas_call(
        _rope_kernel,
        grid=(T // rows, 2 * DIFF_HEADS),
        in_specs=[pl.BlockSpec((rows, LANES), lambda i, j: (i, COL_CQ + j)),
                  pl.BlockSpec((rows, LANES), lambda i, j: (i % ns, 0)),
                  pl.BlockSpec((rows, LANES), lambda i, j: (i % ns, 0))],
        out_specs=pl.BlockSpec((rows, LANES), lambda i, j: (i, j)),
        out_shape=jax.ShapeDtypeStruct((T, 2 * HW), BF16),
        compiler_params=_cparams(("arbitrary", "arbitrary")),
        name="rope",
    )(h, cos_t, sin_t)


def _attn_kernel(qi_ref, kj_ref, q_ref, k_ref, v_ref, lam_ref, g_ref, o_ref,
                 m_ref, l_ref, acc_ref, *, tq, lambda_init):
    p = pl.program_id(2)
    i = qi_ref[p]
    j = kj_ref[p]

    @pl.when(j == 0)
    def _():
        m_ref[...] = jnp.full(m_ref.shape, -jnp.inf, F32)
        l_ref[...] = jnp.zeros(l_ref.shape, F32)
        acc_ref[...] = jnp.zeros(acc_ref.shape, F32)

    def step(masked):
        q = q_ref[...]
        k = k_ref[...]
        v = v_ref[...]
        lane = lax.broadcasted_iota(jnp.int32, q.shape, 1)
        zero = jnp.zeros_like(q)
        qs = (jnp.where(lane < DIFF_QK_DIM, q, zero), jnp.where(lane >= DIFF_QK_DIM, q, zero))
        if masked:
            rows = lax.broadcasted_iota(jnp.int32, (tq, tq), 0)
            cols = lax.broadcasted_iota(jnp.int32, (tq, tq), 1)
            allowed = cols // CHUNK <= rows // CHUNK
        for t in range(2):
            s = lax.dot_general(qs[t], k, (((1,), (1,)), ((), ())), preferred_element_type=F32)
            if masked:
                s = jnp.where(allowed, s, -jnp.inf)
            m_old = m_ref[t]
            m_new = jnp.maximum(m_old, jnp.max(s, axis=-1, keepdims=True))
            alpha = jnp.exp(m_old - m_new)
            pr = jnp.exp(s - m_new)
            l_ref[t] = alpha * l_ref[t] + jnp.sum(pr, axis=-1, keepdims=True)
            acc_ref[t] = alpha * acc_ref[t] + jnp.dot(pr.astype(BF16), v, preferred_element_type=F32)
            m_ref[t] = m_new

    @pl.when(j < i)
    def _():
        step(False)

    @pl.when(j == i)
    def _():
        step(True)
        lam_v = lam_ref[...]
        lam = (jnp.exp(jnp.sum(lam_v[0:1] * lam_v[1:2])) - jnp.exp(jnp.sum(lam_v[2:3] * lam_v[3:4]))
               + lambda_init)
        o = acc_ref[0] / l_ref[0] - lam * (acc_ref[1] / l_ref[1])
        y = o * lax.rsqrt(jnp.mean(o * o, axis=-1, keepdims=True) + RMS_EPS) * g_ref[...]
        o_ref[...] = (y * (1.0 - lambda_init)).astype(o_ref.dtype)


def diff_attention(qk, h, lam_params, norm_g, B, S, lambda_init):
    T = h.shape[0]
    tq = min(512, S)
    nq = S // tq
    pairs = [(i, j) for i in range(nq) for j in range(i + 1)]
    qi = jnp.asarray(np.array([p[0] for p in pairs], np.int32))
    kj = jnp.asarray(np.array([p[1] for p in pairs], np.int32))
    grid_spec = pltpu.PrefetchScalarGridSpec(
        num_scalar_prefetch=2,
        grid=(B, DIFF_HEADS, len(pairs)),
        in_specs=[pl.BlockSpec((tq, LANES), lambda b, hd, p, qi, kj: (b * nq + qi[p], hd)),
                  pl.BlockSpec((tq, LANES), lambda b, hd, p, qi, kj: (b * nq + kj[p], DIFF_HEADS + hd)),
                  pl.BlockSpec((tq, LANES), lambda b, hd, p, qi, kj: (b * nq + kj[p], COL_CV + hd)),
                  pl.BlockSpec((4, DIFF_QK_DIM), lambda b, hd, p, qi, kj: (0, 0)),
                  pl.BlockSpec((1, LANES), lambda b, hd, p, qi, kj: (0, 0))],
        out_specs=pl.BlockSpec((tq, LANES), lambda b, hd, p, qi, kj: (b * nq + qi[p], hd)),
        scratch_shapes=[pltpu.VMEM((2, tq, 1), F32), pltpu.VMEM((2, tq, 1), F32),
                        pltpu.VMEM((2, tq, LANES), F32)],
    )
    return pl.pallas_call(
        functools.partial(_attn_kernel, tq=tq, lambda_init=lambda_init),
        grid_spec=grid_spec,
        out_shape=jax.ShapeDtypeStruct((T, HW), BF16),
        compiler_params=_cparams(("arbitrary", "arbitrary", "arbitrary")),
        name="diff_attn",
    )(qi, kj, qk, qk, h, lam_params, norm_g.reshape(1, LANES))


def _silu(t):
    return t * jax.nn.sigmoid(t)


def _softplus(t):
    return jnp.maximum(t, 0.0) + jnp.log1p(jnp.exp(-jnp.abs(t)))


def _dot32(a, b):
    return jnp.dot(a, b, preferred_element_type=F32, precision=HIGHEST)


def _dn_kernel(q_ref, k_ref, v_ref, gate_ref, ab_ref, cwq_ref, cwk_ref, cwv_ref, alog_ref, dt_ref,
               ng_ref, o_ref, ext_ref, tail_ref, state_ref, *, rows):
    hd = pl.program_id(1)
    r = pl.program_id(2)
    pad = 8

    @pl.when(r == 0)
    def _():
        tail_ref[...] = jnp.zeros(tail_ref.shape, F32)
        state_ref[...] = jnp.zeros(state_ref.shape, F32)

    def conv_silu(x_ref, cw_ref, slot):
        x = x_ref[...].astype(F32)
        ext_ref[0:pad, :] = tail_ref[slot]
        ext_ref[pad:pad + rows, :] = x
        tail_ref[slot] = x[rows - pad:rows]
        cw = cw_ref[...]
        acc = None
        for t in range(DN_CONV):
            off = pad - (DN_CONV - 1) + t
            term = ext_ref[off:off + rows, :] * cw[t:t + 1, :]
            acc = term if acc is None else acc + term
        return _silu(acc)

    def l2n(t):
        return t * lax.rsqrt(jnp.sum(t * t, axis=-1, keepdims=True) + RMS_EPS)

    q = l2n(conv_silu(q_ref, cwq_ref, 0)) * (HEAD_DIM ** -0.5)
    k = l2n(conv_silu(k_ref, cwk_ref, 1))
    v = conv_silu(v_ref, cwv_ref, 2)

    ab = ab_ref[...].astype(F32)
    lane = lax.broadcasted_iota(jnp.int32, ab.shape, 1)
    g_all = -jnp.exp(alog_ref[...]) * _softplus(ab + dt_ref[...])
    beta_all = jax.nn.sigmoid(ab)
    g_col = jnp.sum(jnp.where(lane == hd, g_all, 0.0), axis=-1, keepdims=True)
    beta = jnp.sum(jnp.where(lane == DN_HEADS + hd, beta_all, 0.0), axis=-1, keepdims=True)

    ri = lax.broadcasted_iota(jnp.int32, (CHUNK, CHUNK), 0)
    ci = lax.broadcasted_iota(jnp.int32, (CHUNK, CHUNK), 1)
    causal = ri >= ci
    strict = ri > ci
    tri = causal.astype(F32)
    eye = (ri == ci).astype(F32)

    state = state_ref[...]
    gate = gate_ref[...].astype(F32)
    ng = ng_ref[...]
    for c in range(rows // CHUNK):
        sl = slice(c * CHUNK, (c + 1) * CHUNK)
        qc, kc, vc, bc = q[sl], k[sl], v[sl], beta[sl]
        gcum = _dot32(tri, jnp.broadcast_to(g_col[sl], (CHUNK, LANES)))
        gdiff = gcum[:, :CHUNK] - gcum.T[:CHUNK, :]
        decay = jnp.exp(jnp.where(causal, gdiff, -jnp.inf))
        kb = kc * bc
        a = lax.dot_general(kb, kc, (((1,), (1,)), ((), ())), preferred_element_type=F32, precision=HIGHEST)
        m = jnp.where(strict, a * decay, 0.0)
        tinv = eye - m
        pw = _dot32(m, m)
        for it in range(5):
            tinv = tinv + _dot32(tinv, pw)
            if it < 4:
                pw = _dot32(pw, pw)
        rhs = jnp.concatenate([vc * bc, kb * jnp.exp(gcum)], axis=-1)
        sol = _dot32(tinv, rhs)
        u, w = sol[:, :HEAD_DIM], sol[:, HEAD_DIM:]
        attn = lax.dot_general(qc.astype(BF16), kc.astype(BF16), (((1,), (1,)), ((), ())),
                               preferred_element_type=F32) * decay
        g_last = gcum[CHUNK - 1:CHUNK, :]
        sb = state.astype(BF16)
        v_new = u - jnp.dot(w.astype(BF16), sb, preferred_element_type=F32)
        o = (jnp.dot((qc * jnp.exp(gcum)).astype(BF16), sb, preferred_element_type=F32)
             + jnp.dot(attn.astype(BF16), v_new.astype(BF16), preferred_element_type=F32))
        k_dec = kc * jnp.exp(g_last - gcum)
        state = state * jnp.exp(g_last) + jnp.dot(k_dec.T.astype(BF16), v_new.astype(BF16),
                                                  preferred_element_type=F32)
        y = o * lax.rsqrt(jnp.mean(o * o, axis=-1, keepdims=True) + RMS_EPS) * ng
        o_ref[sl, :] = (y * _silu(gate[sl])).astype(o_ref.dtype)
    state_ref[...] = state


def deltanet(h, conv_w, a_log, dt_bias, norm_g, B, S):
    T = h.shape[0]
    rows = min(256, S)
    nr = S // rows
    alog_row = jnp.zeros((1, LANES), F32).at[0, :DN_HEADS].set(a_log)
    dt_row = jnp.zeros((1, LANES), F32).at[0, :DN_HEADS].set(dt_bias)

    def col(c):
        return pl.BlockSpec((rows, LANES), lambda b, hd, r: (b * nr + r, c + hd))

    def cw(c):
        return pl.BlockSpec((DN_CONV, LANES), lambda b, hd, r: (0, c + hd))

    row = pl.BlockSpec((1, LANES), lambda b, hd, r: (0, 0))
    return pl.pallas_call(
        functools.partial(_dn_kernel, rows=rows),
        grid=(B, DN_HEADS, nr),
        in_specs=[col(COL_BQ), col(COL_BK), col(COL_BV), col(COL_BG),
                  pl.BlockSpec((rows, LANES), lambda b, hd, r: (b * nr + r, COL_AB)),
                  cw(0), cw(DN_HEADS), cw(2 * DN_HEADS), row, row, row],
        out_specs=pl.BlockSpec((rows, LANES), lambda b, hd, r: (b * nr + r, hd)),
        out_shape=jax.ShapeDtypeStruct((T, HW), BF16),
        scratch_shapes=[pltpu.VMEM((rows + 8, LANES), F32), pltpu.VMEM((3, 8, LANES), F32),
                        pltpu.VMEM((HEAD_DIM, HEAD_DIM), F32)],
        compiler_params=_cparams(("arbitrary", "arbitrary", "arbitrary")),
        name="deltanet",
    )(h, h, h, h, h, conv_w, conv_w, conv_w, alog_row, dt_row, norm_g.reshape(1, LANES))


def _layer_norm(z, g, b):
    mu = jnp.mean(z, axis=-1, keepdims=True)
    var = jnp.mean(jnp.square(z - mu), axis=-1, keepdims=True)
    return (z - mu) * lax.rsqrt(var + LN_EPS) * g + b


def _outproj_kernel(ya_ref, yb_ref, yc_ref, x_ref, wa_ref, wb_ref, wc_ref, g_ref, b_ref, rw_ref, rb_ref,
                    x1_ref, rt_ref):
    mixed = (jnp.dot(ya_ref[...], wa_ref[...], preferred_element_type=F32)
             + jnp.dot(yb_ref[...], wb_ref[...], preferred_element_type=F32)
             + jnp.dot(yc_ref[...], wc_ref[...], preferred_element_type=F32))
    x1 = _layer_norm(DEEPNORM_ALPHA * x_ref[...] + mixed, g_ref[...], b_ref[...])
    x1_ref[...] = x1

    scores = jax.nn.sigmoid(_dot32(x1, rw_ref[...]))
    tm = scores.shape[0]
    lane = lax.broadcasted_iota(jnp.int32, scores.shape, 1)
    lanef = lane.astype(F32)
    neg = -jnp.inf
    big = float(LANES)
    sel = jnp.where(lane < N_EXPERTS, scores + rb_ref[...], neg)

    def top2(mg):
        t1 = jnp.max(mg, axis=-1, keepdims=True)
        i1 = jnp.min(jnp.where(mg == t1, lanef, big), axis=-1, keepdims=True)
        mg2 = jnp.where(lanef == i1, neg, mg)
        t2 = jnp.max(mg2, axis=-1, keepdims=True)
        i2 = jnp.min(jnp.where(mg2 == t2, lanef, big), axis=-1, keepdims=True)
        return t1 + t2, i1, i2

    best_s = best_1 = best_2 = None
    for gidx in range(N_EXPERT_GROUPS):
        in_g = (lane >= gidx * EXPERTS_PER_GROUP) & (lane < (gidx + 1) * EXPERTS_PER_GROUP)
        gs, i1, i2 = top2(jnp.where(in_g, sel, neg))
        if best_s is None:
            best_s, best_1, best_2 = gs, i1, i2
        else:
            take = gs > best_s
            best_s = jnp.where(take, gs, best_s)
            best_1 = jnp.where(take, i1, best_1)
            best_2 = jnp.where(take, i2, best_2)
    s1 = jnp.sum(jnp.where(lanef == best_1, scores, 0.0), axis=-1, keepdims=True)
    s2 = jnp.sum(jnp.where(lanef == best_2, scores, 0.0), axis=-1, keepdims=True)
    tot = s1 + s2
    l8 = lax.broadcasted_iota(jnp.int32, (tm, 8), 1)
    rt_ref[...] = jnp.where(l8 == 0, best_1, jnp.where(l8 == 1, best_2, jnp.where(
        l8 == 2, s1 / tot, jnp.where(l8 == 3, s2 / tot, 0.0))))


def out_proj_ln_router(ya, yb, yc, x, w_out, ln_g, ln_b, rw_pad, rb_pad):
    T = x.shape[0]
    tm = min(256, T)
    wa = w_out[:SGU_WIDTH].astype(BF16)
    wb = w_out[SGU_WIDTH:SGU_WIDTH + HW].astype(BF16)
    wc = w_out[SGU_WIDTH + HW:].astype(BF16)

    def rowblk(width):
        return pl.BlockSpec((tm, width), lambda i: (i, 0))

    def full(shape):
        return pl.BlockSpec(shape, lambda i: (0, 0))

    return pl.pallas_call(
        _outproj_kernel,
        grid=(T // tm,),
        in_specs=[rowblk(SGU_WIDTH), rowblk(HW), rowblk(HW), rowblk(D_MODEL),
                  full((SGU_WIDTH, D_MODEL)), full((HW, D_MODEL)), full((HW, D_MODEL)),
                  full((1, D_MODEL)), full((1, D_MODEL)), full((D_MODEL, LANES)), full((1, LANES))],
        out_specs=[rowblk(D_MODEL), rowblk(8)],
        out_shape=[jax.ShapeDtypeStruct((T, D_MODEL), F32), jax.ShapeDtypeStruct((T, 8), F32)],
        compiler_params=_cparams(("arbitrary",)),
        name="out_proj_ln_router",
    )(ya, yb, yc, x, wa, wb, wc, ln_g.reshape(1, D_MODEL), ln_b.reshape(1, D_MODEL), rw_pad, rb_pad)


def _route_tables(eidx, T, nb):
    A = T * TOP_K
    flat_e = eidx.reshape(A)
    order = jnp.argsort(flat_e, stable=True).astype(jnp.int32)
    e_sorted = flat_e[order]
    counts = jnp.zeros((N_EXPERTS,), jnp.int32).at[flat_e].add(1)
    padded = (counts + MOE_BLK - 1) // MOE_BLK * MOE_BLK
    pad_end = jnp.cumsum(padded)
    pad_start = pad_end - padded
    start = jnp.cumsum(counts) - counts
    dest = pad_start[e_sorted] + jnp.arange(A, dtype=jnp.int32) - start[e_sorted]
    P = nb * MOE_BLK
    tok = order // TOP_K
    slot = order % TOP_K
    src = jnp.zeros((P,), jnp.int32).at[dest].set(tok)
    dst = jnp.zeros((P,), jnp.int32).at[dest].set(slot * T + tok)
    blk_start = jnp.arange(nb, dtype=jnp.int32) * MOE_BLK
    e_of_blk = jnp.searchsorted(pad_end, blk_start, side='right').astype(jnp.int32)
    block_expert = jnp.minimum(e_of_blk, N_EXPERTS - 1)
    n_valid = jnp.where(e_of_blk < N_EXPERTS,
                        jnp.clip(pad_start[block_expert] + counts[block_expert] - blk_start, 0, MOE_BLK), 0)
    return block_expert, n_valid.astype(jnp.int32), src.reshape(nb, 1, MOE_BLK), dst.reshape(nb, 1, MOE_BLK)


def _moe_kernel(be_ref, nv_ref, src_ref, dst_ref, x_hbm, wg_ref, wu_ref, wd_ref, y_hbm,
                xbuf, ybuf, gsem, ssem):
    n_valid = nv_ref[pl.program_id(0)]

    def row_gather(r):
        return pltpu.make_async_copy(x_hbm.at[pl.ds(src_ref[0, r], 1), :], xbuf.at[pl.ds(r, 1), :], gsem)

    def row_scatter(r):
        return pltpu.make_async_copy(ybuf.at[pl.ds(r, 1), :], y_hbm.at[pl.ds(dst_ref[0, r], 1), :], ssem)

    def for_rows(n, fn):
        def body(r, carry):
            fn(r)
            return carry
        lax.fori_loop(0, n, body, 0)

    @pl.when(n_valid > 0)
    def _():
        for_rows(MOE_BLK, lambda r: row_gather(r).start())
        for_rows(MOE_BLK, lambda r: row_gather(r).wait())
        xb = xbuf[...].astype(BF16)
        hg = jnp.dot(xb, wg_ref[...], preferred_element_type=F32)
        hu = jnp.dot(xb, wu_ref[...], preferred_element_type=F32)
        hb = (_silu(hg) * hu).astype(BF16)
        ybuf[...] = jnp.dot(hb, wd_ref[...], preferred_element_type=F32)
        for_rows(n_valid, lambda r: row_scatter(r).start())
        for_rows(n_valid, lambda r: row_scatter(r).wait())


def moe(x1, eidx, wg, wu, wd):
    T = x1.shape[0]
    A = T * TOP_K
    nb = -(-A // MOE_BLK) + N_EXPERTS
    block_expert, n_valid, src, dst = _route_tables(eidx, T, nb)
    smem_blk = pl.BlockSpec((None, 1, MOE_BLK), lambda b, be, nv: (b, 0, 0), memory_space=pltpu.SMEM)
    grid_spec = pltpu.PrefetchScalarGridSpec(
        num_scalar_prefetch=2,
        grid=(nb,),
        in_specs=[smem_blk, smem_blk,
                  pl.BlockSpec(memory_space=pl.ANY),
                  pl.BlockSpec((None, D_MODEL, D_EXPERT), lambda b, be, nv: (be[b], 0, 0)),
                  pl.BlockSpec((None, D_MODEL, D_EXPERT), lambda b, be, nv: (be[b], 0, 0)),
                  pl.BlockSpec((None, D_EXPERT, D_MODEL), lambda b, be, nv: (be[b], 0, 0))],
        out_specs=pl.BlockSpec(memory_space=pl.ANY),
        scratch_shapes=[pltpu.VMEM((MOE_BLK, D_MODEL), F32), pltpu.VMEM((MOE_BLK, D_MODEL), F32),
                        pltpu.SemaphoreType.DMA(()), pltpu.SemaphoreType.DMA(())],
    )
    return pl.pallas_call(
        _moe_kernel,
        grid_spec=grid_spec,
        out_shape=jax.ShapeDtypeStruct((A, D_MODEL), F32),
        compiler_params=_cparams(("arbitrary",)),
        name="moe",
    )(block_expert, n_valid, src, dst, x1, wg, wu, wd)


def _combine_kernel(x_ref, y0_ref, y1_ref, rt_ref, g_ref, b_ref, o_ref, ob_ref):
    rt = rt_ref[...]
    ffn = y0_ref[...] * rt[:, 2:3] + y1_ref[...] * rt[:, 3:4]
    x2 = _layer_norm(DEEPNORM_ALPHA * x_ref[...] + ffn, g_ref[...], b_ref[...])
    o_ref[...] = x2
    ob_ref[...] = x2.astype(BF16)


def combine_ln(x1, y, rt, ln_g, ln_b):
    T = x1.shape[0]
    tm = min(256, T)
    nt = T // tm
    blk = pl.BlockSpec((tm, D_MODEL), lambda i: (i, 0))
    vec = pl.BlockSpec((1, D_MODEL), lambda i: (0, 0))
    return pl.pallas_call(
        _combine_kernel,
        grid=(nt,),
        in_specs=[blk, blk, pl.BlockSpec((tm, D_MODEL), lambda i: (nt + i, 0)),
                  pl.BlockSpec((tm, 8), lambda i: (i, 0)), vec, vec],
        out_specs=[blk, blk],
        out_shape=[jax.ShapeDtypeStruct((T, D_MODEL), F32), jax.ShapeDtypeStruct((T, D_MODEL), BF16)],
        compiler_params=_cparams(("arbitrary",)),
        name="combine_ln",
    )(x1, y, y, rt, ln_g.reshape(1, D_MODEL), ln_b.reshape(1, D_MODEL))


def _repack_w_in(w):
    sizes = [SGU_WIDTH, SGU_WIDTH, HW, HW, HW, HW, DN_HEADS, DN_HEADS, HW, HW, HW]
    offs = np.concatenate([[0], np.cumsum(sizes)])
    a_u, a_v, b_q, b_k, b_v, b_g, b_a, b_b, c_q, c_k, c_v = (
        w[:, offs[n]:offs[n + 1]] for n in range(len(sizes)))
    padz = jnp.zeros((w.shape[0], LANES - 2 * DN_HEADS), w.dtype)
    return jnp.concatenate([b_q, b_k, b_v, b_g, c_q, c_k, c_v, a_u, a_v, b_a, b_b, padz], axis=1).astype(BF16)


def kernel(x, w_in, sgu_norm_g, sgu_norm_b, sgu_w, sgu_b, dn_conv_w, dn_a_log, dn_dt_bias, dn_norm_g,
           diff_lambda_q1, diff_lambda_k1, diff_lambda_q2, diff_lambda_k2, diff_norm_g, w_out, ln1_g, ln1_b,
           router_w, router_bias, moe_w_gate, moe_w_up, moe_w_down, ln2_g, ln2_b):
    B, S, D = x.shape
    T = B * S
    xf = x.reshape(T, D)
    xb = xf.astype(BF16)
    cos_t, sin_t = _rope_tables(S)
    rw_pad = jnp.zeros((D, LANES), F32).at[:, :N_EXPERTS].set(router_w)
    rb_pad = jnp.zeros((1, LANES), F32).at[0, :N_EXPERTS].set(router_bias)
    for l in range(DEPTH):
        h = in_proj(xb, _repack_w_in(w_in[l]))
        y_a = sgu(h, sgu_norm_g[l], sgu_norm_b[l], sgu_w[l], sgu_b[l])
        y_b = deltanet(h, dn_conv_w[l], dn_a_log[l], dn_dt_bias[l], dn_norm_g[l], B, S)
        qk = rope(h, cos_t, sin_t)
        lam_params = jnp.stack([diff_lambda_q1[l], diff_lambda_k1[l], diff_lambda_q2[l], diff_lambda_k2[l]])
        lambda_init = 0.8 - 0.6 * math.exp(-0.3 * l)
        y_c = diff_attention(qk, h, lam_params, diff_norm_g[l], B, S, lambda_init)
        x1, rt = out_proj_ln_router(y_a, y_b, y_c, xf, w_out[l], ln1_g[l], ln1_b[l], rw_pad, rb_pad)
        eidx = rt[:, :TOP_K].astype(jnp.int32)
        y = moe(x1, eidx, moe_w_gate[l].astype(BF16), moe_w_up[l].astype(BF16), moe_w_down[l].astype(BF16))
        xf, xb = combine_ln(x1, y, rt, ln2_g[l], ln2_b[l])
    return xf.reshape(B, S, D)
```

```python
import functools
import math

import numpy as np
import jax
import jax.numpy as jnp
from jax import lax
from jax.experimental import pallas as pl
from jax.experimental.pallas import tpu as pltpu

F32 = jnp.float32
BF16 = jnp.bfloat16
HIGHEST = lax.Precision.HIGHEST

D_MODEL = 2048
DEPTH = 2
CHUNK = 64
HEAD_DIM = 128
SGU_GROUPS = 4
SGU_WIDTH = SGU_GROUPS * HEAD_DIM
SGU_BLOCK = 128
DN_HEADS = 6
DN_CONV = 4
DIFF_HEADS = 6
DIFF_QK_DIM = 64
ROPE_THETA = 10000.0
N_EXPERTS = 32
N_EXPERT_GROUPS = 4
EXPERTS_PER_GROUP = N_EXPERTS // N_EXPERT_GROUPS
TOP_K = 2
D_EXPERT = D_MODEL // 2
DEEPNORM_ALPHA = (2 * DEPTH) ** 0.25
LN_EPS = 1e-5
RMS_EPS = 1e-6

LANES = 128
VMEM_LIMIT = 56 * 1024 * 1024

HW = DN_HEADS * HEAD_DIM
COL_BQ, COL_BK, COL_BV, COL_BG = 0, 6, 12, 18
COL_CQ, COL_CK, COL_CV = 24, 30, 36
COL_AU, COL_AV = 42, 46
COL_AB = 50
NW = 51 * LANES
N_TILES_IN = 3

MOE_BLK = 256


def _cparams(sem, vmem=VMEM_LIMIT):
    return pltpu.CompilerParams(dimension_semantics=sem, vmem_limit_bytes=vmem)


def _matmul_kernel(x_ref, w_ref, o_ref):
    o_ref[...] = jnp.dot(x_ref[...], w_ref[...], preferred_element_type=F32).astype(o_ref.dtype)


def in_proj(xb, w):
    T = xb.shape[0]
    tm = min(512, T)
    tn = NW // N_TILES_IN
    return pl.pallas_call(
        _matmul_kernel,
        grid=(N_TILES_IN, T // tm),
        in_specs=[pl.BlockSpec((tm, D_MODEL), lambda j, i: (i, 0)),
                  pl.BlockSpec((D_MODEL, tn), lambda j, i: (0, j))],
        out_specs=pl.BlockSpec((tm, tn), lambda j, i: (i, j)),
        out_shape=jax.ShapeDtypeStruct((T, NW), BF16),
        compiler_params=_cparams(("arbitrary", "arbitrary")),
        name="in_proj",
    )(xb, w)


def _gelu(t):
    return t * (lax.erf(t * (2.0 ** -0.5)) + 1.0) * 0.5


def _sgu_kernel(u_ref, v_ref, ng_ref, nb_ref, w_ref, b_ref, o_ref, *, rows):
    u = _gelu(u_ref[...].astype(F32))
    v = _gelu(v_ref[...].astype(F32))
    mu = jnp.mean(v, axis=-1, keepdims=True)
    var = jnp.mean(jnp.square(v - mu), axis=-1, keepdims=True)
    vn = ((v - mu) * lax.rsqrt(var + LN_EPS) * ng_ref[...] + nb_ref[...]).astype(BF16)
    ii = lax.broadcasted_iota(jnp.int32, (SGU_BLOCK, SGU_BLOCK), 0)
    jj = lax.broadcasted_iota(jnp.int32, (SGU_BLOCK, SGU_BLOCK), 1)
    w = jnp.where(jj // CHUNK <= ii // CHUNK, w_ref[...], 0.0).astype(BF16)
    bias = b_ref[...]
    for n in range(rows // SGU_BLOCK):
        sl = slice(n * SGU_BLOCK, (n + 1) * SGU_BLOCK)
        s = jnp.dot(w, vn[sl], preferred_element_type=F32) + bias
        o_ref[sl, :] = (u[sl] * s).astype(o_ref.dtype)


def sgu(h, norm_g, norm_b, w_s, b_s):
    T = h.shape[0]
    rows = min(512, T)
    return pl.pallas_call(
        functools.partial(_sgu_kernel, rows=rows),
        grid=(T // rows, SGU_GROUPS),
        in_specs=[pl.BlockSpec((rows, LANES), lambda i, g: (i, COL_AU + g)),
                  pl.BlockSpec((rows, LANES), lambda i, g: (i, COL_AV + g)),
                  pl.BlockSpec((1, LANES), lambda i, g: (0, g)),
                  pl.BlockSpec((1, LANES), lambda i, g: (0, g)),
                  pl.BlockSpec((None, SGU_BLOCK, SGU_BLOCK), lambda i, g: (g, 0, 0)),
                  pl.BlockSpec((None, SGU_BLOCK, 1), lambda i, g: (g, 0, 0))],
        out_specs=pl.BlockSpec((rows, LANES), lambda i, g: (i, g)),
        out_shape=jax.ShapeDtypeStruct((T, SGU_WIDTH), BF16),
        compiler_params=_cparams(("arbitrary", "arbitrary")),
        name="sgu",
    )(h, h, norm_g.reshape(1, SGU_WIDTH), norm_b.reshape(1, SGU_WIDTH), w_s,
      b_s.reshape(SGU_GROUPS, SGU_BLOCK, 1))


def _rope_kernel(x_ref, cos_ref, sin_ref, o_ref):
    x = x_ref[...].astype(F32)
    lane = lax.broadcasted_iota(jnp.int32, x.shape, 1)
    half = DIFF_QK_DIM // 2
    partner = jnp.where(lane % DIFF_QK_DIM < half,
                        pltpu.roll(x, LANES - half, axis=1), pltpu.roll(x, half, axis=1))
    scale = jnp.where(pl.program_id(1) < DIFF_HEADS, DIFF_QK_DIM ** -0.5, 1.0)
    o_ref[...] = ((x * cos_ref[...] + partner * sin_ref[...]) * scale).astype(o_ref.dtype)


def _rope_tables(S):
    half = DIFF_QK_DIM // 2
    inv_freq = ROPE_THETA ** (-jnp.arange(half, dtype=F32) / half)
    ang = jnp.arange(S, dtype=F32)[:, None] * inv_freq[None, :]
    cos, sin = jnp.cos(ang), jnp.sin(ang)
    cos_t = jnp.tile(cos, (1, LANES // half))
    sin_t = jnp.tile(jnp.concatenate([-sin, sin], axis=1), (1, LANES // DIFF_QK_DIM))
    return cos_t, sin_t


def rope(h, cos_t, sin_t):
    T = h.shape[0]
    S = cos_t.shape[0]
    rows = min(512, S)
    ns = S // rows
    return pl.pallas_call(
        _rope_kernel,
        grid=(T // rows, 2 * DIFF_HEADS),
        in_specs=[pl.BlockSpec((rows, LANES), lambda i, j: (i, COL_CQ + j)),
                  pl.BlockSpec((rows, LANES), lambda i, j: (i % ns, 0)),
                  pl.BlockSpec((rows, LANES), lambda i, j: (i % ns, 0))],
        out_specs=pl.BlockSpec((rows, LANES), lambda i, j: (i, j)),
        out_shape=jax.ShapeDtypeStruct((T, 2 * HW), BF16),
        compiler_params=_cparams(("arbitrary", "arbitrary")),
        name="rope",
    )(h, cos_t, sin_t)


ATT_TILE = 1024
VT_ROWS = HEAD_DIM + 16


def _vprep_kernel(v_ref, o_ref):
    vt = v_ref[...].astype(F32).T
    o_ref[0:HEAD_DIM, :] = vt.astype(o_ref.dtype)
    row = lax.broadcasted_iota(jnp.int32, (VT_ROWS - HEAD_DIM, vt.shape[1]), 0)
    o_ref[HEAD_DIM:VT_ROWS, :] = jnp.where(row == 0, 1.0, 0.0).astype(o_ref.dtype)


def vprep(h, B, S):
    tk = min(ATT_TILE, S)
    nk = S // tk
    return pl.pallas_call(
        _vprep_kernel,
        grid=(B, DIFF_HEADS, nk),
        in_specs=[pl.BlockSpec((tk, LANES), lambda b, hd, j: (b * nk + j, COL_CV + hd))],
        out_specs=pl.BlockSpec((None, None, VT_ROWS, tk), lambda b, hd, j: (b * DIFF_HEADS + hd, j, 0, 0)),
        out_shape=jax.ShapeDtypeStruct((B * DIFF_HEADS, nk, VT_ROWS, tk), BF16),
        compiler_params=_cparams(("arbitrary", "arbitrary", "arbitrary")),
        name="vprep",
    )(h)


def _attn_kernel(q_ref, k_ref, vt_ref, lam_ref, g_ref, o_ref, acc_ref, *, tq, lambda_init):
    i = pl.program_id(2)
    q = q_ref[...]
    lane = lax.broadcasted_iota(jnp.int32, q.shape, 1)
    zero = jnp.zeros_like(q)
    qs = (jnp.where(lane < DIFF_QK_DIM, q, zero), jnp.where(lane >= DIFF_QK_DIM, q, zero))
    acc_ref[...] = jnp.zeros(acc_ref.shape, F32)
    neg = jnp.full((1, tq), -jnp.inf, F32)

    def block(j, ms, masked):
        kb = k_ref[pl.ds(pl.multiple_of(j * tq, tq), tq), :]
        vb = vt_ref[j]
        s = [lax.dot_general(kb, qs[t], (((1,), (1,)), ((), ())), preferred_element_type=F32)
             for t in range(2)]
        if masked:
            kr = lax.broadcasted_iota(jnp.int32, (tq, tq), 0)
            qc = lax.broadcasted_iota(jnp.int32, (tq, tq), 1)
            allowed = kr // CHUNK <= qc // CHUNK
        new = []
        for t in range(2):
            st = jnp.where(allowed, s[t], -jnp.inf) if masked else s[t]
            m_new = jnp.maximum(ms[t], jnp.max(st, axis=0, keepdims=True))
            alpha = jnp.exp(ms[t] - m_new)
            p = jnp.exp(st - m_new).astype(BF16)
            acc_ref[t] = alpha * acc_ref[t] + jnp.dot(vb, p, preferred_element_type=F32)
            new.append(m_new)
        return tuple(new)

    ms = lax.fori_loop(0, i, lambda j, ms: block(j, ms, False), (neg, neg))
    block(i, ms, True)

    lam_v = lam_ref[...]
    lam = (jnp.exp(jnp.sum(lam_v[0:1] * lam_v[1:2])) - jnp.exp(jnp.sum(lam_v[2:3] * lam_v[3:4]))
           + lambda_init)
    a0 = acc_ref[0]
    a1 = acc_ref[1]
    o = a0[:HEAD_DIM] / a0[HEAD_DIM:HEAD_DIM + 1] - lam * (a1[:HEAD_DIM] / a1[HEAD_DIM:HEAD_DIM + 1])
    y = o * lax.rsqrt(jnp.mean(o * o, axis=0, keepdims=True) + RMS_EPS) * g_ref[...]
    o_ref[...] = (y * (1.0 - lambda_init)).T.astype(o_ref.dtype)


def diff_attention(qk, vt, lam_params, norm_g, B, S, lambda_init):
    T = qk.shape[0]
    tq = min(ATT_TILE, S)
    nq = S // tq
    return pl.pallas_call(
        functools.partial(_attn_kernel, tq=tq, lambda_init=lambda_init),
        grid=(B, DIFF_HEADS, nq),
        in_specs=[pl.BlockSpec((tq, LANES), lambda b, hd, i: (b * nq + i, hd)),
                  pl.BlockSpec((S, LANES), lambda b, hd, i: (b, DIFF_HEADS + hd)),
                  pl.BlockSpec((None, nq, VT_ROWS, tq), lambda b, hd, i: (b * DIFF_HEADS + hd, 0, 0, 0)),
                  pl.BlockSpec((4, DIFF_QK_DIM), lambda b, hd, i: (0, 0)),
                  pl.BlockSpec((HEAD_DIM, 1), lambda b, hd, i: (0, 0))],
        out_specs=pl.BlockSpec((tq, LANES), lambda b, hd, i: (b * nq + i, hd)),
        out_shape=jax.ShapeDtypeStruct((T, HW), BF16),
        scratch_shapes=[pltpu.VMEM((2, VT_ROWS, tq), F32)],
        compiler_params=_cparams(("arbitrary", "arbitrary", "arbitrary")),
        name="diff_attn",
    )(qk, qk, vt, lam_params, norm_g.reshape(HEAD_DIM, 1))


def _silu(t):
    return t * jax.nn.sigmoid(t)


def _softplus(t):
    return jnp.maximum(t, 0.0) + jnp.log1p(jnp.exp(-jnp.abs(t)))


def _dot32(a, b):
    return jnp.dot(a, b, preferred_element_type=F32, precision=HIGHEST)


def _mm(a, b):
    return jnp.dot(a.astype(BF16), b.astype(BF16), preferred_element_type=F32)


def _mm_nt(a, b):
    return lax.dot_general(a.astype(BF16), b.astype(BF16), (((1,), (1,)), ((), ())),
                           preferred_element_type=F32)


DN_ROWS = 512
DN_HB = 2


def _gates_kernel(ab_ref, alog_ref, dt_ref, o_ref, *, rows):
    ab = ab_ref[...].astype(F32)
    g_all = -jnp.exp(alog_ref[...]) * _softplus(ab + dt_ref[...])
    beta_all = jax.nn.sigmoid(ab)
    ri = lax.broadcasted_iota(jnp.int32, (CHUNK, CHUNK), 0)
    ci = lax.broadcasted_iota(jnp.int32, (CHUNK, CHUNK), 1)
    tri = (ri >= ci).astype(F32)
    is_decay_lane = lax.broadcasted_iota(jnp.int32, (CHUNK, LANES), 1) < DN_HEADS
    for c in range(rows // CHUNK):
        sl = slice(c * CHUNK, (c + 1) * CHUNK)
        o_ref[sl, :] = jnp.where(is_decay_lane, _dot32(tri, g_all[sl]), beta_all[sl])


def dn_gates(h, a_log, dt_bias):
    T = h.shape[0]
    rows = min(512, T)
    alog_row = jnp.zeros((1, LANES), F32).at[0, :DN_HEADS].set(a_log)
    dt_row = jnp.zeros((1, LANES), F32).at[0, :DN_HEADS].set(dt_bias)
    row = pl.BlockSpec((1, LANES), lambda i: (0, 0))
    return pl.pallas_call(
        functools.partial(_gates_kernel, rows=rows),
        grid=(T // rows,),
        in_specs=[pl.BlockSpec((rows, LANES), lambda i: (i, COL_AB)), row, row],
        out_specs=pl.BlockSpec((rows, LANES), lambda i: (i, 0)),
        out_shape=jax.ShapeDtypeStruct((T, LANES), F32),
        compiler_params=_cparams(("arbitrary",)),
        name="dn_gates",
    )(h, alog_row, dt_row)


def _dn_kernel(q_ref, k_ref, v_ref, gate_ref, gb_ref, cwq_ref, cwk_ref, cwv_ref, ng_ref, o_ref,
               ext_ref, tail_ref, state_ref, *, rows, hb):
    hd0 = pl.program_id(1) * hb
    r = pl.program_id(2)
    pad = 8
    nc = rows // CHUNK

    @pl.when(r == 0)
    def _():
        tail_ref[...] = jnp.zeros(tail_ref.shape, F32)
        state_ref[...] = jnp.zeros(state_ref.shape, F32)

    def conv_silu(x_ref, cw_ref, slot):
        x = x_ref[...].astype(F32)
        ext_ref[0:pad, :] = tail_ref[slot]
        ext_ref[pad:pad + rows, :] = x
        tail_ref[slot] = x[rows - pad:rows]
        cw = cw_ref[...]
        acc = None
        for t in range(DN_CONV):
            off = pad - (DN_CONV - 1) + t
            term = ext_ref[off:off + rows, :] * cw[t:t + 1, :]
            acc = term if acc is None else acc + term
        return _silu(acc)

    def l2n(t):
        return t * lax.rsqrt(jnp.sum(t * t, axis=-1, keepdims=True) + RMS_EPS)

    qf = conv_silu(q_ref, cwq_ref, 0)
    kf = conv_silu(k_ref, cwk_ref, 1)
    vf = conv_silu(v_ref, cwv_ref, 2)
    gb = gb_ref[...]
    lane = lax.broadcasted_iota(jnp.int32, gb.shape, 1)

    ri = lax.broadcasted_iota(jnp.int32, (CHUNK, CHUNK), 0)
    ci = lax.broadcasted_iota(jnp.int32, (CHUNK, CHUNK), 1)
    causal = ri >= ci
    strict = ri > ci
    eye = (ri == ci).astype(F32)

    units = []
    for hh in range(hb):
        hl = slice(hh * HEAD_DIM, (hh + 1) * HEAD_DIM)
        q = l2n(qf[:, hl]) * (HEAD_DIM ** -0.5)
        k = l2n(kf[:, hl])
        gc_col = jnp.sum(jnp.where(lane == hd0 + hh, gb, 0.0), axis=-1, keepdims=True)
        beta = jnp.sum(jnp.where(lane == hd0 + hh + DN_HEADS, gb, 0.0), axis=-1, keepdims=True)
        gcb = jnp.broadcast_to(gc_col, (rows, HEAD_DIM))
        eg = jnp.exp(gcb)
        kb = k * beta
        rhs = jnp.concatenate([vf[:, hl] * beta, kb * eg], axis=-1)
        qg = q * eg
        for c in range(nc):
            sl = slice(c * CHUNK, (c + 1) * CHUNK)
            units.append(dict(q=q[sl], k=k[sl], kb=kb[sl], rhs=rhs[sl], qg=qg[sl], g=gcb[sl]))
    for un in units:
        g = un["g"]
        gdiff = g[:, :CHUNK] - g.T[:CHUNK, :]
        un["decay"] = jnp.exp(jnp.where(causal, gdiff, -jnp.inf))
    for un in units:
        un["m"] = jnp.where(strict, _mm_nt(un["kb"], un["k"]) * un["decay"], 0.0)
    for un in units:
        un["tinv"] = eye - un["m"]
        un["pw"] = _mm(un["m"], un["m"])
    for it in range(5):
        for un in units:
            un["tinv"] = un["tinv"] + _mm(un["tinv"], un["pw"])
        if it < 4:
            for un in units:
                un["pw"] = _mm(un["pw"], un["pw"])
    for un in units:
        un["sol"] = _mm(un["tinv"], un["rhs"])
        un["attn"] = _mm_nt(un["q"], un["k"]) * un["decay"]

    gate = gate_ref[...].astype(F32)
    ng = ng_ref[...]
    states = [state_ref[hh] for hh in range(hb)]
    for c in range(nc):
        sl = slice(c * CHUNK, (c + 1) * CHUNK)
        for hh in range(hb):
            un = units[hh * nc + c]
            hl = slice(hh * HEAD_DIM, (hh + 1) * HEAD_DIM)
            u, w = un["sol"][:, :HEAD_DIM], un["sol"][:, HEAD_DIM:]
            g = un["g"]
            g_last = g[CHUNK - 1:CHUNK, :]
            sb = states[hh].astype(BF16)
            v_new = u - jnp.dot(w.astype(BF16), sb, preferred_element_type=F32)
            vnb = v_new.astype(BF16)
            o = (jnp.dot(un["qg"].astype(BF16), sb, preferred_element_type=F32)
                 + jnp.dot(un["attn"].astype(BF16), vnb, preferred_element_type=F32))
            k_dec = un["k"] * jnp.exp(g_last - g)
            states[hh] = states[hh] * jnp.exp(g_last) + jnp.dot(k_dec.T.astype(BF16), vnb,
                                                                preferred_element_type=F32)
            y = o * lax.rsqrt(jnp.mean(o * o, axis=-1, keepdims=True) + RMS_EPS) * ng
            o_ref[sl, hl] = (y * _silu(gate[sl, hl])).astype(o_ref.dtype)
    for hh in range(hb):
        state_ref[hh] = states[hh]


def deltanet(h, gb, conv_w, norm_g, B, S):
    T = h.shape[0]
    rows = min(DN_ROWS, S)
    nr = S // rows
    hb = DN_HB
    width = hb * HEAD_DIM

    def col(c):
        return pl.BlockSpec((rows, width), lambda b, hg, r: (b * nr + r, c // hb + hg))

    def cw(c):
        return pl.BlockSpec((DN_CONV, width), lambda b, hg, r: (0, c // hb + hg))

    return pl.pallas_call(
        functools.partial(_dn_kernel, rows=rows, hb=hb),
        grid=(B, DN_HEADS // hb, nr),
        in_specs=[col(COL_BQ), col(COL_BK), col(COL_BV), col(COL_BG),
                  pl.BlockSpec((rows, LANES), lambda b, hg, r: (b * nr + r, 0)),
                  cw(0), cw(DN_HEADS), cw(2 * DN_HEADS),
                  pl.BlockSpec((1, LANES), lambda b, hg, r: (0, 0))],
        out_specs=pl.BlockSpec((rows, width), lambda b, hg, r: (b * nr + r, hg)),
        out_shape=jax.ShapeDtypeStruct((T, HW), BF16),
        scratch_shapes=[pltpu.VMEM((rows + 8, width), F32), pltpu.VMEM((3, 8, width), F32),
                        pltpu.VMEM((hb, HEAD_DIM, HEAD_DIM), F32)],
        compiler_params=_cparams(("arbitrary", "arbitrary", "arbitrary")),
        name="deltanet",
    )(h, h, h, h, gb, conv_w, conv_w, conv_w, norm_g.reshape(1, LANES))


def _layer_norm(z, g, b):
    mu = jnp.mean(z, axis=-1, keepdims=True)
    var = jnp.mean(jnp.square(z - mu), axis=-1, keepdims=True)
    return (z - mu) * lax.rsqrt(var + LN_EPS) * g + b


def _outproj_kernel(ya_ref, yb_ref, yc_ref, x_ref, wa_ref, wb_ref, wc_ref, g_ref, b_ref, rw_ref, rb_ref,
                    x1_ref, rt_ref):
    mixed = (jnp.dot(ya_ref[...], wa_ref[...], preferred_element_type=F32)
             + jnp.dot(yb_ref[...], wb_ref[...], preferred_element_type=F32)
             + jnp.dot(yc_ref[...], wc_ref[...], preferred_element_type=F32))
    x1 = _layer_norm(DEEPNORM_ALPHA * x_ref[...] + mixed, g_ref[...], b_ref[...])
    x1_ref[...] = x1

    scores = jax.nn.sigmoid(_dot32(x1, rw_ref[...]))
    tm = scores.shape[0]
    lane = lax.broadcasted_iota(jnp.int32, scores.shape, 1)
    lanef = lane.astype(F32)
    neg = -jnp.inf
    big = float(LANES)
    sel = jnp.where(lane < N_EXPERTS, scores + rb_ref[...], neg)

    def top2(mg):
        t1 = jnp.max(mg, axis=-1, keepdims=True)
        i1 = jnp.min(jnp.where(mg == t1, lanef, big), axis=-1, keepdims=True)
        mg2 = jnp.where(lanef == i1, neg, mg)
        t2 = jnp.max(mg2, axis=-1, keepdims=True)
        i2 = jnp.min(jnp.where(mg2 == t2, lanef, big), axis=-1, keepdims=True)
        return t1 + t2, i1, i2

    best_s = best_1 = best_2 = None
    for gidx in range(N_EXPERT_GROUPS):
        in_g = (lane >= gidx * EXPERTS_PER_GROUP) & (lane < (gidx + 1) * EXPERTS_PER_GROUP)
        gs, i1, i2 = top2(jnp.where(in_g, sel, neg))
        if best_s is None:
            best_s, best_1, best_2 = gs, i1, i2
        else:
            take = gs > best_s
            best_s = jnp.where(take, gs, best_s)
            best_1 = jnp.where(take, i1, best_1)
            best_2 = jnp.where(take, i2, best_2)
    s1 = jnp.sum(jnp.where(lanef == best_1, scores, 0.0), axis=-1, keepdims=True)
    s2 = jnp.sum(jnp.where(lanef == best_2, scores, 0.0), axis=-1, keepdims=True)
    tot = s1 + s2
    l8 = lax.broadcasted_iota(jnp.int32, (tm, 8), 1)
    rt_ref[...] = jnp.where(l8 == 0, best_1, jnp.where(l8 == 1, best_2, jnp.where(
        l8 == 2, s1 / tot, jnp.where(l8 == 3, s2 / tot, 0.0))))


def out_proj_ln_router(ya, yb, yc, x, w_out, ln_g, ln_b, rw_pad, rb_pad):
    T = x.shape[0]
    tm = min(256, T)
    wa = w_out[:SGU_WIDTH].astype(BF16)
    wb = w_out[SGU_WIDTH:SGU_WIDTH + HW].astype(BF16)
    wc = w_out[SGU_WIDTH + HW:].astype(BF16)

    def rowblk(width):
        return pl.BlockSpec((tm, width), lambda i: (i, 0))

    def full(shape):
        return pl.BlockSpec(shape, lambda i: (0, 0))

    return pl.pallas_call(
        _outproj_kernel,
        grid=(T // tm,),
        in_specs=[rowblk(SGU_WIDTH), rowblk(HW), rowblk(HW), rowblk(D_MODEL),
                  full((SGU_WIDTH, D_MODEL)), full((HW, D_MODEL)), full((HW, D_MODEL)),
                  full((1, D_MODEL)), full((1, D_MODEL)), full((D_MODEL, LANES)), full((1, LANES))],
        out_specs=[rowblk(D_MODEL), rowblk(8)],
        out_shape=[jax.ShapeDtypeStruct((T, D_MODEL), F32), jax.ShapeDtypeStruct((T, 8), F32)],
        compiler_params=_cparams(("arbitrary",)),
        name="out_proj_ln_router",
    )(ya, yb, yc, x, wa, wb, wc, ln_g.reshape(1, D_MODEL), ln_b.reshape(1, D_MODEL), rw_pad, rb_pad)


def _route_tables(eidx, T, nb):
    A = T * TOP_K
    flat_e = eidx.reshape(A)
    order = jnp.argsort(flat_e, stable=True).astype(jnp.int32)
    experts = jnp.arange(N_EXPERTS, dtype=jnp.int32)
    counts = jnp.sum((flat_e[:, None] == experts[None, :]).astype(jnp.int32), axis=0)
    padded = (counts + MOE_BLK - 1) // MOE_BLK * MOE_BLK
    pad_end = jnp.cumsum(padded)
    pad_start = pad_end - padded
    start = jnp.cumsum(counts) - counts
    pos = jnp.arange(nb * MOE_BLK, dtype=jnp.int32)
    e_of_pos = jnp.sum((pos[:, None] >= pad_end[None, :]).astype(jnp.int32), axis=1)
    onehot = e_of_pos[:, None] == experts[None, :]

    def lookup(tab):
        return jnp.sum(jnp.where(onehot, tab[None, :], 0), axis=1)

    rank = pos - lookup(pad_start)
    valid = rank < lookup(counts)
    a_sorted = order[jnp.clip(lookup(start) + rank, 0, A - 1)]
    tok = a_sorted // TOP_K
    slot = a_sorted % TOP_K
    src = jnp.where(valid, tok, 0)
    dst = jnp.where(valid, slot * T + tok, 0)
    block_expert = jnp.minimum(e_of_pos.reshape(nb, MOE_BLK)[:, 0], N_EXPERTS - 1)
    n_valid = jnp.sum(valid.reshape(nb, MOE_BLK).astype(jnp.int32), axis=1)
    return block_expert, n_valid, src.reshape(nb, 1, MOE_BLK), dst.reshape(nb, 1, MOE_BLK)


def _moe_kernel(be_ref, nv_ref, src_ref, dst_ref, x_hbm, wg_ref, wu_ref, wd_ref, y_hbm,
                xbuf, ybuf, gsem, ssem):
    n_valid = nv_ref[pl.program_id(0)]

    def row_gather(r):
        return pltpu.make_async_copy(x_hbm.at[pl.ds(src_ref[0, r], 1), :], xbuf.at[pl.ds(r, 1), :], gsem)

    def row_scatter(r):
        return pltpu.make_async_copy(ybuf.at[pl.ds(r, 1), :], y_hbm.at[pl.ds(dst_ref[0, r], 1), :], ssem)

    def for_rows(n, fn):
        def body(r, carry):
            fn(r)
            return carry
        lax.fori_loop(0, n, body, 0)

    @pl.when(n_valid > 0)
    def _():
        for_rows(MOE_BLK, lambda r: row_gather(r).start())
        for_rows(MOE_BLK, lambda r: row_gather(r).wait())
        xb = xbuf[...].astype(BF16)
        hg = jnp.dot(xb, wg_ref[...], preferred_element_type=F32)
        hu = jnp.dot(xb, wu_ref[...], preferred_element_type=F32)
        hb = (_silu(hg) * hu).astype(BF16)
        ybuf[...] = jnp.dot(hb, wd_ref[...], preferred_element_type=F32)
        for_rows(n_valid, lambda r: row_scatter(r).start())
        for_rows(n_valid, lambda r: row_scatter(r).wait())


def moe(x1, eidx, wg, wu, wd):
    T = x1.shape[0]
    A = T * TOP_K
    nb = -(-A // MOE_BLK) + N_EXPERTS
    block_expert, n_valid, src, dst = _route_tables(eidx, T, nb)
    smem_blk = pl.BlockSpec((None, 1, MOE_BLK), lambda b, be, nv: (b, 0, 0), memory_space=pltpu.SMEM)
    grid_spec = pltpu.PrefetchScalarGridSpec(
        num_scalar_prefetch=2,
        grid=(nb,),
        in_specs=[smem_blk, smem_blk,
                  pl.BlockSpec(memory_space=pl.ANY),
                  pl.BlockSpec((None, D_MODEL, D_EXPERT), lambda b, be, nv: (be[b], 0, 0)),
                  pl.BlockSpec((None, D_MODEL, D_EXPERT), lambda b, be, nv: (be[b], 0, 0)),
                  pl.BlockSpec((None, D_EXPERT, D_MODEL), lambda b, be, nv: (be[b], 0, 0))],
        out_specs=pl.BlockSpec(memory_space=pl.ANY),
        scratch_shapes=[pltpu.VMEM((MOE_BLK, D_MODEL), F32), pltpu.VMEM((MOE_BLK, D_MODEL), F32),
                        pltpu.SemaphoreType.DMA(()), pltpu.SemaphoreType.DMA(())],
    )
    return pl.pallas_call(
        _moe_kernel,
        grid_spec=grid_spec,
        out_shape=jax.ShapeDtypeStruct((A, D_MODEL), F32),
        compiler_params=_cparams(("arbitrary",)),
        name="moe",
    )(block_expert, n_valid, src, dst, x1, wg, wu, wd)


def _combine_kernel(x_ref, y0_ref, y1_ref, rt_ref, g_ref, b_ref, o_ref, ob_ref):
    rt = rt_ref[...]
    ffn = y0_ref[...] * rt[:, 2:3] + y1_ref[...] * rt[:, 3:4]
    x2 = _layer_norm(DEEPNORM_ALPHA * x_ref[...] + ffn, g_ref[...], b_ref[...])
    o_ref[...] = x2
    ob_ref[...] = x2.astype(BF16)


def combine_ln(x1, y, rt, ln_g, ln_b):
    T = x1.shape[0]
    tm = min(256, T)
    nt = T // tm
    blk = pl.BlockSpec((tm, D_MODEL), lambda i: (i, 0))
    vec = pl.BlockSpec((1, D_MODEL), lambda i: (0, 0))
    return pl.pallas_call(
        _combine_kernel,
        grid=(nt,),
        in_specs=[blk, blk, pl.BlockSpec((tm, D_MODEL), lambda i: (nt + i, 0)),
                  pl.BlockSpec((tm, 8), lambda i: (i, 0)), vec, vec],
        out_specs=[blk, blk],
        out_shape=[jax.ShapeDtypeStruct((T, D_MODEL), F32), jax.ShapeDtypeStruct((T, D_MODEL), BF16)],
        compiler_params=_cparams(("arbitrary",)),
        name="combine_ln",
    )(x1, y, y, rt, ln_g.reshape(1, D_MODEL), ln_b.reshape(1, D_MODEL))


def _repack_w_in(w):
    sizes = [SGU_WIDTH, SGU_WIDTH, HW, HW, HW, HW, DN_HEADS, DN_HEADS, HW, HW, HW]
    offs = np.concatenate([[0], np.cumsum(sizes)])
    a_u, a_v, b_q, b_k, b_v, b_g, b_a, b_b, c_q, c_k, c_v = (
        w[:, offs[n]:offs[n + 1]] for n in range(len(sizes)))
    padz = jnp.zeros((w.shape[0], LANES - 2 * DN_HEADS), w.dtype)
    return jnp.concatenate([b_q, b_k, b_v, b_g, c_q, c_k, c_v, a_u, a_v, b_a, b_b, padz], axis=1).astype(BF16)


def kernel(x, w_in, sgu_norm_g, sgu_norm_b, sgu_w, sgu_b, dn_conv_w, dn_a_log, dn_dt_bias, dn_norm_g,
           diff_lambda_q1, diff_lambda_k1, diff_lambda_q2, diff_lambda_k2, diff_norm_g, w_out, ln1_g, ln1_b,
           router_w, router_bias, moe_w_gate, moe_w_up, moe_w_down, ln2_g, ln2_b):
    B, S, D = x.shape
    T = B * S
    xf = x.reshape(T, D)
    xb = xf.astype(BF16)
    cos_t, sin_t = _rope_tables(S)
    rw_pad = jnp.zeros((D, LANES), F32).at[:, :N_EXPERTS].set(router_w)
    rb_pad = jnp.zeros((1, LANES), F32).at[0, :N_EXPERTS].set(router_bias)
    for l in range(DEPTH):
        h = in_proj(xb, _repack_w_in(w_in[l]))
        y_a = sgu(h, sgu_norm_g[l], sgu_norm_b[l], sgu_w[l], sgu_b[l])
        y_b = deltanet(h, dn_gates(h, dn_a_log[l], dn_dt_bias[l]), dn_conv_w[l], dn_norm_g[l], B, S)
        qk = rope(h, cos_t, sin_t)
        lam_params = jnp.stack([diff_lambda_q1[l], diff_lambda_k1[l], diff_lambda_q2[l], diff_lambda_k2[l]])
        lambda_init = 0.8 - 0.6 * math.exp(-0.3 * l)
        y_c = diff_attention(qk, vprep(h, B, S), lam_params, diff_norm_g[l], B, S, lambda_init)
        x1, rt = out_proj_ln_router(y_a, y_b, y_c, xf, w_out[l], ln1_g[l], ln1_b[l], rw_pad, rb_pad)
        eidx = rt[:, :TOP_K].astype(jnp.int32)
        y = moe(x1, eidx, moe_w_gate[l].astype(BF16), moe_w_up[l].astype(BF16), moe_w_down[l].astype(BF16))
        xf, xb = combine_ln(x1, y, rt, ln2_g[l], ln2_b[l])
    return xf.reshape(B, S, D)
```

```python
import functools
import math

import numpy as np
import jax
import jax.numpy as jnp
from jax import lax
from jax.experimental import pallas as pl
from jax.experimental.pallas import tpu as pltpu

F32 = jnp.float32
BF16 = jnp.bfloat16
HIGHEST = lax.Precision.HIGHEST

D_MODEL = 2048
DEPTH = 2
CHUNK = 64
HEAD_DIM = 128
SGU_GROUPS = 4
SGU_WIDTH = SGU_GROUPS * HEAD_DIM
SGU_BLOCK = 128
DN_HEADS = 6
DN_CONV = 4
DIFF_HEADS = 6
DIFF_QK_DIM = 64
ROPE_THETA = 10000.0
N_EXPERTS = 32
N_EXPERT_GROUPS = 4
EXPERTS_PER_GROUP = N_EXPERTS // N_EXPERT_GROUPS
TOP_K = 2
D_EXPERT = D_MODEL // 2
DEEPNORM_ALPHA = (2 * DEPTH) ** 0.25
LN_EPS = 1e-5
RMS_EPS = 1e-6

LANES = 128
VMEM_LIMIT = 56 * 1024 * 1024

HW = DN_HEADS * HEAD_DIM
COL_AU, COL_AV = 0, 4
COL_BQ, COL_BK, COL_BV, COL_BG = 8, 14, 20, 26
COL_CQ, COL_CK, COL_CV = 32, 38, 44
COL_AB = 50
AB_SRC_COL = 2 * SGU_WIDTH + 4 * HW
NW = 51 * LANES
N_TILES_IN = 3

MOE_BLK = 256


def _cparams(sem, vmem=VMEM_LIMIT):
    return pltpu.CompilerParams(dimension_semantics=sem, vmem_limit_bytes=vmem)


def _matmul_kernel(x_ref, w_ref, o_ref):
    o_ref[...] = jnp.dot(x_ref[...], w_ref[...], preferred_element_type=F32).astype(o_ref.dtype)


def in_proj(xb, w):
    T = xb.shape[0]
    tm = min(512, T)
    tn = NW // N_TILES_IN
    return pl.pallas_call(
        _matmul_kernel,
        grid=(N_TILES_IN, T // tm),
        in_specs=[pl.BlockSpec((tm, D_MODEL), lambda j, i: (i, 0)),
                  pl.BlockSpec((D_MODEL, tn), lambda j, i: (0, j))],
        out_specs=pl.BlockSpec((tm, tn), lambda j, i: (i, j)),
        out_shape=jax.ShapeDtypeStruct((T, NW), BF16),
        compiler_params=_cparams(("arbitrary", "arbitrary")),
        name="in_proj",
    )(xb, w)


def _gelu(t):
    return t * (lax.erf(t * (2.0 ** -0.5)) + 1.0) * 0.5


def _sgu_kernel(u_ref, v_ref, ng_ref, nb_ref, w_ref, b_ref, o_ref, *, rows):
    u = _gelu(u_ref[...].astype(F32))
    v = _gelu(v_ref[...].astype(F32))
    mu = jnp.mean(v, axis=-1, keepdims=True)
    var = jnp.mean(jnp.square(v - mu), axis=-1, keepdims=True)
    vn = ((v - mu) * lax.rsqrt(var + LN_EPS) * ng_ref[...] + nb_ref[...]).astype(BF16)
    ii = lax.broadcasted_iota(jnp.int32, (SGU_BLOCK, SGU_BLOCK), 0)
    jj = lax.broadcasted_iota(jnp.int32, (SGU_BLOCK, SGU_BLOCK), 1)
    w = jnp.where(jj // CHUNK <= ii // CHUNK, w_ref[...], 0.0).astype(BF16)
    bias = b_ref[...]
    for n in range(rows // SGU_BLOCK):
        sl = slice(n * SGU_BLOCK, (n + 1) * SGU_BLOCK)
        s = jnp.dot(w, vn[sl], preferred_element_type=F32) + bias
        o_ref[sl, :] = (u[sl] * s).astype(o_ref.dtype)


def sgu(h, norm_g, norm_b, w_s, b_s):
    T = h.shape[0]
    rows = min(512, T)
    return pl.pallas_call(
        functools.partial(_sgu_kernel, rows=rows),
        grid=(T // rows, SGU_GROUPS),
        in_specs=[pl.BlockSpec((rows, LANES), lambda i, g: (i, COL_AU + g)),
                  pl.BlockSpec((rows, LANES), lambda i, g: (i, COL_AV + g)),
                  pl.BlockSpec((1, LANES), lambda i, g: (0, g)),
                  pl.BlockSpec((1, LANES), lambda i, g: (0, g)),
                  pl.BlockSpec((None, SGU_BLOCK, SGU_BLOCK), lambda i, g: (g, 0, 0)),
                  pl.BlockSpec((None, SGU_BLOCK, 1), lambda i, g: (g, 0, 0))],
        out_specs=pl.BlockSpec((rows, LANES), lambda i, g: (i, g)),
        out_shape=jax.ShapeDtypeStruct((T, SGU_WIDTH), BF16),
        compiler_params=_cparams(("arbitrary", "arbitrary")),
        name="sgu",
    )(h, h, norm_g.reshape(1, SGU_WIDTH), norm_b.reshape(1, SGU_WIDTH), w_s,
      b_s.reshape(SGU_GROUPS, SGU_BLOCK, 1))


def _rope_kernel(x_ref, cos_ref, sin_ref, o_ref):
    x = x_ref[...].astype(F32)
    lane = lax.broadcasted_iota(jnp.int32, x.shape, 1)
    half = DIFF_QK_DIM // 2
    partner = jnp.where(lane % DIFF_QK_DIM < half,
                        pltpu.roll(x, LANES - half, axis=1), pltpu.roll(x, half, axis=1))
    scale = jnp.where(pl.program_id(1) < DIFF_HEADS, DIFF_QK_DIM ** -0.5 * math.log2(math.e), 1.0)
    o_ref[...] = ((x * cos_ref[...] + partner * sin_ref[...]) * scale).astype(o_ref.dtype)


def _rope_tables(S):
    half = DIFF_QK_DIM // 2
    inv_freq = ROPE_THETA ** (-jnp.arange(half, dtype=F32) / half)
    ang = jnp.arange(S, dtype=F32)[:, None] * inv_freq[None, :]
    cos, sin = jnp.cos(ang), jnp.sin(ang)
    cos_t = jnp.tile(cos, (1, LANES // half))
    sin_t = jnp.tile(jnp.concatenate([-sin, sin], axis=1), (1, LANES // DIFF_QK_DIM))
    return cos_t, sin_t


def rope(h, cos_t, sin_t):
    T = h.shape[0]
    S = cos_t.shape[0]
    rows = min(512, S)
    ns = S // rows
    return pl.pallas_call(
        _rope_kernel,
        grid=(T // rows, 2 * DIFF_HEADS),
        in_specs=[pl.BlockSpec((rows, LANES), lambda i, j: (i, COL_CQ + j)),
                  pl.BlockSpec((rows, LANES), lambda i, j: (i % ns, 0)),
                  pl.BlockSpec((rows, LANES), lambda i, j: (i % ns, 0))],
        out_specs=pl.BlockSpec((rows, LANES), lambda i, j: (i, j)),
        out_shape=jax.ShapeDtypeStruct((T, 2 * HW), BF16),
        compiler_params=_cparams(("arbitrary", "arbitrary")),
        name="rope",
    )(h, cos_t, sin_t)


ATT_TILE = 1024
VT_ROWS = HEAD_DIM + 16


def _vprep_kernel(v_ref, o_ref):
    vt = v_ref[...].astype(F32).T
    o_ref[0:HEAD_DIM, :] = vt.astype(o_ref.dtype)
    row = lax.broadcasted_iota(jnp.int32, (VT_ROWS - HEAD_DIM, vt.shape[1]), 0)
    o_ref[HEAD_DIM:VT_ROWS, :] = jnp.where(row == 0, 1.0, 0.0).astype(o_ref.dtype)


def vprep(h, B, S):
    tk = min(ATT_TILE, S)
    nk = S // tk
    return pl.pallas_call(
        _vprep_kernel,
        grid=(B, DIFF_HEADS, nk),
        in_specs=[pl.BlockSpec((tk, LANES), lambda b, hd, j: (b * nk + j, COL_CV + hd))],
        out_specs=pl.BlockSpec((None, None, VT_ROWS, tk), lambda b, hd, j: (b * DIFF_HEADS + hd, j, 0, 0)),
        out_shape=jax.ShapeDtypeStruct((B * DIFF_HEADS, nk, VT_ROWS, tk), BF16),
        compiler_params=_cparams(("arbitrary", "arbitrary", "arbitrary")),
        name="vprep",
    )(h)


def _attn_kernel(q_ref, k_ref, vt_ref, lam_ref, g_ref, o_ref, acc_ref, *, tq, lambda_init):
    i = pl.program_id(2)
    q = q_ref[...]
    lane = lax.broadcasted_iota(jnp.int32, q.shape, 1)
    zero = jnp.zeros_like(q)
    qs = (jnp.where(lane < DIFF_QK_DIM, q, zero), jnp.where(lane >= DIFF_QK_DIM, q, zero))
    acc_ref[...] = jnp.zeros(acc_ref.shape, F32)
    neg = jnp.full((1, tq), -jnp.inf, F32)

    def block(j, ms, masked):
        kb = k_ref[pl.ds(pl.multiple_of(j * tq, tq), tq), :]
        vb = vt_ref[j]
        s = [lax.dot_general(kb, qs[t], (((1,), (1,)), ((), ())), preferred_element_type=F32)
             for t in range(2)]
        if masked:
            kr = lax.broadcasted_iota(jnp.int32, (tq, tq), 0)
            qc = lax.broadcasted_iota(jnp.int32, (tq, tq), 1)
            allowed = kr // CHUNK <= qc // CHUNK
        new = []
        for t in range(2):
            st = jnp.where(allowed, s[t], -jnp.inf) if masked else s[t]
            m_new = jnp.maximum(ms[t], jnp.max(st, axis=0, keepdims=True))
            alpha = jnp.exp2(ms[t] - m_new)
            p = jnp.exp2(st - m_new).astype(BF16)
            acc_ref[t] = alpha * acc_ref[t] + jnp.dot(vb, p, preferred_element_type=F32)
            new.append(m_new)
        return tuple(new)

    ms = lax.fori_loop(0, i, lambda j, ms: block(j, ms, False), (neg, neg))
    block(i, ms, True)

    lam_v = lam_ref[...]
    lam = (jnp.exp(jnp.sum(lam_v[0:1] * lam_v[1:2])) - jnp.exp(jnp.sum(lam_v[2:3] * lam_v[3:4]))
           + lambda_init)
    a0 = acc_ref[0]
    a1 = acc_ref[1]
    o = a0[:HEAD_DIM] / a0[HEAD_DIM:HEAD_DIM + 1] - lam * (a1[:HEAD_DIM] / a1[HEAD_DIM:HEAD_DIM + 1])
    y = o * lax.rsqrt(jnp.mean(o * o, axis=0, keepdims=True) + RMS_EPS) * g_ref[...]
    o_ref[...] = (y * (1.0 - lambda_init)).T.astype(o_ref.dtype)


def diff_attention(qk, vt, lam_params, norm_g, B, S, lambda_init):
    T = qk.shape[0]
    tq = min(ATT_TILE, S)
    nq = S // tq
    return pl.pallas_call(
        functools.partial(_attn_kernel, tq=tq, lambda_init=lambda_init),
        grid=(B, DIFF_HEADS, nq),
        in_specs=[pl.BlockSpec((tq, LANES), lambda b, hd, i: (b * nq + i, hd)),
                  pl.BlockSpec((S, LANES), lambda b, hd, i: (b, DIFF_HEADS + hd)),
                  pl.BlockSpec((None, nq, VT_ROWS, tq), lambda b, hd, i: (b * DIFF_HEADS + hd, 0, 0, 0)),
                  pl.BlockSpec((4, DIFF_QK_DIM), lambda b, hd, i: (0, 0)),
                  pl.BlockSpec((HEAD_DIM, 1), lambda b, hd, i: (0, 0))],
        out_specs=pl.BlockSpec((tq, LANES), lambda b, hd, i: (b * nq + i, hd)),
        out_shape=jax.ShapeDtypeStruct((T, HW), BF16),
        scratch_shapes=[pltpu.VMEM((2, VT_ROWS, tq), F32)],
        compiler_params=_cparams(("arbitrary", "arbitrary", "arbitrary")),
        name="diff_attn",
    )(qk, qk, vt, lam_params, norm_g.reshape(HEAD_DIM, 1))


def _silu(t):
    return t * jax.nn.sigmoid(t)


def _softplus(t):
    return jnp.maximum(t, 0.0) + jnp.log1p(jnp.exp(-jnp.abs(t)))


def _dot32(a, b):
    return jnp.dot(a, b, preferred_element_type=F32, precision=HIGHEST)


def _mm(a, b):
    return jnp.dot(a.astype(BF16), b.astype(BF16), preferred_element_type=F32)


def _mm_nt(a, b):
    return lax.dot_general(a.astype(BF16), b.astype(BF16), (((1,), (1,)), ((), ())),
                           preferred_element_type=F32)


DN_ROWS = 512
DN_HB = 2


def _gates_kernel(ab_ref, alog_ref, dt_ref, o_ref, *, rows):
    ab = ab_ref[...].astype(F32)
    g_all = -jnp.exp(alog_ref[...]) * _softplus(ab + dt_ref[...])
    beta_all = jax.nn.sigmoid(ab)
    ri = lax.broadcasted_iota(jnp.int32, (CHUNK, CHUNK), 0)
    ci = lax.broadcasted_iota(jnp.int32, (CHUNK, CHUNK), 1)
    tri = (ri >= ci).astype(F32)
    is_decay_lane = lax.broadcasted_iota(jnp.int32, (CHUNK, LANES), 1) < DN_HEADS
    for c in range(rows // CHUNK):
        sl = slice(c * CHUNK, (c + 1) * CHUNK)
        o_ref[sl, :] = jnp.where(is_decay_lane, _dot32(tri, g_all[sl]), beta_all[sl])


def dn_gates(h, a_log, dt_bias):
    T = h.shape[0]
    rows = min(512, T)
    alog_row = jnp.zeros((1, LANES), F32).at[0, :DN_HEADS].set(a_log)
    dt_row = jnp.zeros((1, LANES), F32).at[0, :DN_HEADS].set(dt_bias)
    row = pl.BlockSpec((1, LANES), lambda i: (0, 0))
    return pl.pallas_call(
        functools.partial(_gates_kernel, rows=rows),
        grid=(T // rows,),
        in_specs=[pl.BlockSpec((rows, LANES), lambda i: (i, COL_AB)), row, row],
        out_specs=pl.BlockSpec((rows, LANES), lambda i: (i, 0)),
        out_shape=jax.ShapeDtypeStruct((T, LANES), F32),
        compiler_params=_cparams(("arbitrary",)),
        name="dn_gates",
    )(h, alog_row, dt_row)


def _dn_kernel(q_ref, k_ref, v_ref, gate_ref, gb_ref, cwq_ref, cwk_ref, cwv_ref, ng_ref, o_ref,
               ext_ref, tail_ref, state_ref, *, rows, hb):
    hd0 = pl.program_id(1) * hb
    r = pl.program_id(2)
    pad = 8
    nc = rows // CHUNK

    @pl.when(r == 0)
    def _():
        tail_ref[...] = jnp.zeros(tail_ref.shape, F32)
        state_ref[...] = jnp.zeros(state_ref.shape, F32)

    def conv_silu(x_ref, cw_ref, slot):
        x = x_ref[...].astype(F32)
        ext_ref[0:pad, :] = tail_ref[slot]
        ext_ref[pad:pad + rows, :] = x
        tail_ref[slot] = x[rows - pad:rows]
        cw = cw_ref[...]
        acc = None
        for t in range(DN_CONV):
            off = pad - (DN_CONV - 1) + t
            term = ext_ref[off:off + rows, :] * cw[t:t + 1, :]
            acc = term if acc is None else acc + term
        return _silu(acc)

    def l2n(t):
        return t * lax.rsqrt(jnp.sum(t * t, axis=-1, keepdims=True) + RMS_EPS)

    qf = conv_silu(q_ref, cwq_ref, 0)
    kf = conv_silu(k_ref, cwk_ref, 1)
    vf = conv_silu(v_ref, cwv_ref, 2)
    gb = gb_ref[...]
    lane = lax.broadcasted_iota(jnp.int32, gb.shape, 1)

    ri = lax.broadcasted_iota(jnp.int32, (CHUNK, CHUNK), 0)
    ci = lax.broadcasted_iota(jnp.int32, (CHUNK, CHUNK), 1)
    causal = ri >= ci
    strict = ri > ci
    eye = (ri == ci).astype(F32)

    units = []
    for hh in range(hb):
        hl = slice(hh * HEAD_DIM, (hh + 1) * HEAD_DIM)
        q = l2n(qf[:, hl]) * (HEAD_DIM ** -0.5)
        k = l2n(kf[:, hl])
        gc_col = jnp.sum(jnp.where(lane == hd0 + hh, gb, 0.0), axis=-1, keepdims=True)
        beta = jnp.sum(jnp.where(lane == hd0 + hh + DN_HEADS, gb, 0.0), axis=-1, keepdims=True)
        gcb = jnp.broadcast_to(gc_col, (rows, HEAD_DIM))
        eg = jnp.exp(gcb)
        kb = k * beta
        rhs = jnp.concatenate([vf[:, hl] * beta, kb * eg], axis=-1)
        qg = q * eg
        for c in range(nc):
            sl = slice(c * CHUNK, (c + 1) * CHUNK)
            units.append(dict(q=q[sl], k=k[sl], kb=kb[sl], rhs=rhs[sl], qg=qg[sl], g=gcb[sl]))
    for un in units:
        g = un["g"]
        gdiff = g[:, :CHUNK] - g.T[:CHUNK, :]
        un["decay"] = jnp.exp(jnp.where(causal, gdiff, -jnp.inf))
    for un in units:
        un["m"] = jnp.where(strict, _mm_nt(un["kb"], un["k"]) * un["decay"], 0.0)
    for un in units:
        un["tinv"] = eye - un["m"]
        un["pw"] = _mm(un["m"], un["m"])
    for it in range(5):
        for un in units:
            un["tinv"] = un["tinv"] + _mm(un["tinv"], un["pw"])
        if it < 4:
            for un in units:
                un["pw"] = _mm(un["pw"], un["pw"])
    for un in units:
        un["sol"] = _mm(un["tinv"], un["rhs"])
        un["attn"] = _mm_nt(un["q"], un["k"]) * un["decay"]

    gate = gate_ref[...].astype(F32)
    ng = ng_ref[...]
    states = [state_ref[hh] for hh in range(hb)]
    for c in range(nc):
        sl = slice(c * CHUNK, (c + 1) * CHUNK)
        for hh in range(hb):
            un = units[hh * nc + c]
            hl = slice(hh * HEAD_DIM, (hh + 1) * HEAD_DIM)
            u, w = un["sol"][:, :HEAD_DIM], un["sol"][:, HEAD_DIM:]
            g = un["g"]
            g_last = g[CHUNK - 1:CHUNK, :]
            sb = states[hh].astype(BF16)
            v_new = u - jnp.dot(w.astype(BF16), sb, preferred_element_type=F32)
            vnb = v_new.astype(BF16)
            o = (jnp.dot(un["qg"].astype(BF16), sb, preferred_element_type=F32)
                 + jnp.dot(un["attn"].astype(BF16), vnb, preferred_element_type=F32))
            k_dec = un["k"] * jnp.exp(g_last - g)
            states[hh] = states[hh] * jnp.exp(g_last) + jnp.dot(k_dec.T.astype(BF16), vnb,
                                                                preferred_element_type=F32)
            y = o * lax.rsqrt(jnp.mean(o * o, axis=-1, keepdims=True) + RMS_EPS) * ng
            o_ref[sl, hl] = (y * _silu(gate[sl, hl])).astype(o_ref.dtype)
    for hh in range(hb):
        state_ref[hh] = states[hh]


def deltanet(h, gb, conv_w, norm_g, B, S):
    T = h.shape[0]
    rows = min(DN_ROWS, S)
    nr = S // rows
    hb = DN_HB
    width = hb * HEAD_DIM

    def col(c):
        return pl.BlockSpec((rows, width), lambda b, hg, r: (b * nr + r, c // hb + hg))

    def cw(c):
        return pl.BlockSpec((DN_CONV, width), lambda b, hg, r: (0, c // hb + hg))

    return pl.pallas_call(
        functools.partial(_dn_kernel, rows=rows, hb=hb),
        grid=(B, DN_HEADS // hb, nr),
        in_specs=[col(COL_BQ), col(COL_BK), col(COL_BV), col(COL_BG),
                  pl.BlockSpec((rows, LANES), lambda b, hg, r: (b * nr + r, 0)),
                  cw(0), cw(DN_HEADS), cw(2 * DN_HEADS),
                  pl.BlockSpec((1, LANES), lambda b, hg, r: (0, 0))],
        out_specs=pl.BlockSpec((rows, width), lambda b, hg, r: (b * nr + r, hg)),
        out_shape=jax.ShapeDtypeStruct((T, HW), BF16),
        scratch_shapes=[pltpu.VMEM((rows + 8, width), F32), pltpu.VMEM((3, 8, width), F32),
                        pltpu.VMEM((hb, HEAD_DIM, HEAD_DIM), F32)],
        compiler_params=_cparams(("arbitrary", "arbitrary", "arbitrary")),
        name="deltanet",
    )(h, h, h, h, gb, conv_w, conv_w, conv_w, norm_g.reshape(1, LANES))


def _layer_norm(z, g, b):
    mu = jnp.mean(z, axis=-1, keepdims=True)
    var = jnp.mean(jnp.square(z - mu), axis=-1, keepdims=True)
    return (z - mu) * lax.rsqrt(var + LN_EPS) * g + b


def _outproj_kernel(ya_ref, yb_ref, yc_ref, x_ref, wa_ref, wb_ref, wc_ref, g_ref, b_ref, rw_ref, rb_ref,
                    x1_ref, rt_ref):
    mixed = (jnp.dot(ya_ref[...], wa_ref[...], preferred_element_type=F32)
             + jnp.dot(yb_ref[...], wb_ref[...], preferred_element_type=F32)
             + jnp.dot(yc_ref[...], wc_ref[...], preferred_element_type=F32))
    x1 = _layer_norm(DEEPNORM_ALPHA * x_ref[...] + mixed, g_ref[...], b_ref[...])
    x1_ref[...] = x1

    scores = jax.nn.sigmoid(_dot32(x1, rw_ref[...]))
    tm = scores.shape[0]
    lane = lax.broadcasted_iota(jnp.int32, scores.shape, 1)
    lanef = lane.astype(F32)
    neg = -jnp.inf
    big = float(LANES)
    sel = jnp.where(lane < N_EXPERTS, scores + rb_ref[...], neg)

    def top2(mg):
        t1 = jnp.max(mg, axis=-1, keepdims=True)
        i1 = jnp.min(jnp.where(mg == t1, lanef, big), axis=-1, keepdims=True)
        mg2 = jnp.where(lanef == i1, neg, mg)
        t2 = jnp.max(mg2, axis=-1, keepdims=True)
        i2 = jnp.min(jnp.where(mg2 == t2, lanef, big), axis=-1, keepdims=True)
        return t1 + t2, i1, i2

    best_s = best_1 = best_2 = None
    for gidx in range(N_EXPERT_GROUPS):
        in_g = (lane >= gidx * EXPERTS_PER_GROUP) & (lane < (gidx + 1) * EXPERTS_PER_GROUP)
        gs, i1, i2 = top2(jnp.where(in_g, sel, neg))
        if best_s is None:
            best_s, best_1, best_2 = gs, i1, i2
        else:
            take = gs > best_s
            best_s = jnp.where(take, gs, best_s)
            best_1 = jnp.where(take, i1, best_1)
            best_2 = jnp.where(take, i2, best_2)
    s1 = jnp.sum(jnp.where(lanef == best_1, scores, 0.0), axis=-1, keepdims=True)
    s2 = jnp.sum(jnp.where(lanef == best_2, scores, 0.0), axis=-1, keepdims=True)
    tot = s1 + s2
    l8 = lax.broadcasted_iota(jnp.int32, (tm, 8), 1)
    rt_ref[...] = jnp.where(l8 == 0, best_1, jnp.where(l8 == 1, best_2, jnp.where(
        l8 == 2, s1 / tot, jnp.where(l8 == 3, s2 / tot, 0.0))))


def out_proj_ln_router(ya, yb, yc, x, w_out, ln_g, ln_b, rw_pad, rb_pad):
    T = x.shape[0]
    tm = min(256, T)
    wa = w_out[:SGU_WIDTH].astype(BF16)
    wb = w_out[SGU_WIDTH:SGU_WIDTH + HW].astype(BF16)
    wc = w_out[SGU_WIDTH + HW:].astype(BF16)

    def rowblk(width):
        return pl.BlockSpec((tm, width), lambda i: (i, 0))

    def full(shape):
        return pl.BlockSpec(shape, lambda i: (0, 0))

    return pl.pallas_call(
        _outproj_kernel,
        grid=(T // tm,),
        in_specs=[rowblk(SGU_WIDTH), rowblk(HW), rowblk(HW), rowblk(D_MODEL),
                  full((SGU_WIDTH, D_MODEL)), full((HW, D_MODEL)), full((HW, D_MODEL)),
                  full((1, D_MODEL)), full((1, D_MODEL)), full((D_MODEL, LANES)), full((1, LANES))],
        out_specs=[rowblk(D_MODEL), rowblk(8)],
        out_shape=[jax.ShapeDtypeStruct((T, D_MODEL), F32), jax.ShapeDtypeStruct((T, 8), F32)],
        compiler_params=_cparams(("arbitrary",)),
        name="out_proj_ln_router",
    )(ya, yb, yc, x, wa, wb, wc, ln_g.reshape(1, D_MODEL), ln_b.reshape(1, D_MODEL), rw_pad, rb_pad)


def _route_tables(eidx, T, nb):
    A = T * TOP_K
    flat_e = eidx.reshape(A)
    order = jnp.argsort(flat_e, stable=True).astype(jnp.int32)
    experts = jnp.arange(N_EXPERTS, dtype=jnp.int32)
    counts = jnp.sum((flat_e[:, None] == experts[None, :]).astype(jnp.int32), axis=0)
    padded = (counts + MOE_BLK - 1) // MOE_BLK * MOE_BLK
    pad_end = jnp.cumsum(padded)
    pad_start = pad_end - padded
    start = jnp.cumsum(counts) - counts
    pos = jnp.arange(nb * MOE_BLK, dtype=jnp.int32)
    e_of_pos = jnp.sum((pos[:, None] >= pad_end[None, :]).astype(jnp.int32), axis=1)
    onehot = e_of_pos[:, None] == experts[None, :]

    def lookup(tab):
        return jnp.sum(jnp.where(onehot, tab[None, :], 0), axis=1)

    rank = pos - lookup(pad_start)
    valid = rank < lookup(counts)
    a_sorted = order[jnp.clip(lookup(start) + rank, 0, A - 1)]
    tok = a_sorted // TOP_K
    slot = a_sorted % TOP_K
    src = jnp.where(valid, tok, 0)
    dst = jnp.where(valid, slot * T + tok, 0)
    block_expert = jnp.minimum(e_of_pos.reshape(nb, MOE_BLK)[:, 0], N_EXPERTS - 1)
    n_valid = jnp.sum(valid.reshape(nb, MOE_BLK).astype(jnp.int32), axis=1)
    return block_expert, n_valid, src.reshape(nb, 1, MOE_BLK), dst.reshape(nb, 1, MOE_BLK)


def _moe_kernel(be_ref, nv_ref, src_ref, srcn_ref, dst_ref, dstp_ref, x_hbm, wg_ref, wu_ref, wd_ref, y_hbm,
                xbuf, ybuf, gsem, ssem):
    b = pl.program_id(0)
    slot = b % 2
    nv_cur = nv_ref[b]
    nv_next = nv_ref[b + 1]
    nv_prev2 = nv_ref[jnp.maximum(b - 2, 0)]

    def row_gather(idx_ref, s, r):
        return pltpu.make_async_copy(x_hbm.at[pl.ds(idx_ref[0, r], 1), :], xbuf.at[s, pl.ds(r, 1), :],
                                     gsem.at[s])

    def row_scatter(idx_ref, s, r):
        return pltpu.make_async_copy(ybuf.at[s, pl.ds(r, 1), :], y_hbm.at[pl.ds(idx_ref[0, r], 1), :],
                                     ssem.at[s])

    def for_rows(n, fn):
        def body(r, carry):
            fn(r)
            return carry
        if isinstance(n, int):
            lax.fori_loop(0, n, body, 0, unroll=8)
        else:
            lax.fori_loop(0, n, body, 0)

    @pl.when(jnp.logical_and(b == 0, nv_cur > 0))
    def _():
        for_rows(MOE_BLK, lambda r: row_gather(src_ref, 0, r).start())

    @pl.when(nv_next > 0)
    def _():
        for_rows(MOE_BLK, lambda r: row_gather(srcn_ref, 1 - slot, r).start())

    @pl.when(jnp.logical_and(b >= 2, nv_prev2 > 0))
    def _():
        for_rows(nv_prev2, lambda r: row_scatter(dstp_ref, slot, r).wait())

    @pl.when(nv_cur > 0)
    def _():
        for_rows(MOE_BLK, lambda r: row_gather(src_ref, slot, r).wait())
        xb = xbuf[slot].astype(BF16)
        hg = jnp.dot(xb, wg_ref[...], preferred_element_type=F32)
        hu = jnp.dot(xb, wu_ref[...], preferred_element_type=F32)
        hb = (_silu(hg) * hu).astype(BF16)
        ybuf[slot] = jnp.dot(hb, wd_ref[...], preferred_element_type=F32)
        for_rows(nv_cur, lambda r: row_scatter(dst_ref, slot, r).start())


def moe(x1, eidx, wg, wu, wd, layer):
    T = x1.shape[0]
    A = T * TOP_K
    nb = -(-A // MOE_BLK) + N_EXPERTS
    drain = 2
    block_expert, n_valid, src, dst = _route_tables(eidx, T, nb)
    block_expert = jnp.concatenate([block_expert, jnp.broadcast_to(block_expert[-1:], (drain,))])
    n_valid = jnp.concatenate([n_valid, jnp.zeros((drain + 1,), jnp.int32)])

    def idx_blk(shift):
        return pl.BlockSpec((None, 1, MOE_BLK), lambda b, be, nv: (jnp.clip(b + shift, 0, nb - 1), 0, 0),
                            memory_space=pltpu.SMEM)

    def w_blk(rows, cols):
        return pl.BlockSpec((None, None, rows, cols), lambda b, be, nv: (layer, be[b], 0, 0))

    grid_spec = pltpu.PrefetchScalarGridSpec(
        num_scalar_prefetch=2,
        grid=(nb + drain,),
        in_specs=[idx_blk(0), idx_blk(1), idx_blk(0), idx_blk(-2),
                  pl.BlockSpec(memory_space=pl.ANY),
                  w_blk(D_MODEL, D_EXPERT), w_blk(D_MODEL, D_EXPERT), w_blk(D_EXPERT, D_MODEL)],
        out_specs=pl.BlockSpec(memory_space=pl.ANY),
        scratch_shapes=[pltpu.VMEM((2, MOE_BLK, D_MODEL), F32), pltpu.VMEM((2, MOE_BLK, D_MODEL), F32),
                        pltpu.SemaphoreType.DMA((2,)), pltpu.SemaphoreType.DMA((2,))],
    )
    return pl.pallas_call(
        _moe_kernel,
        grid_spec=grid_spec,
        out_shape=jax.ShapeDtypeStruct((A, D_MODEL), F32),
        compiler_params=_cparams(("arbitrary",)),
        name="moe",
    )(block_expert, n_valid, src, src, dst, dst, x1, wg, wu, wd)


def _combine_kernel(x_ref, y0_ref, y1_ref, rt_ref, g_ref, b_ref, o_ref, ob_ref):
    rt = rt_ref[...]
    ffn = y0_ref[...] * rt[:, 2:3] + y1_ref[...] * rt[:, 3:4]
    x2 = _layer_norm(DEEPNORM_ALPHA * x_ref[...] + ffn, g_ref[...], b_ref[...])
    o_ref[...] = x2
    ob_ref[...] = x2.astype(BF16)


def combine_ln(x1, y, rt, ln_g, ln_b):
    T = x1.shape[0]
    tm = min(256, T)
    nt = T // tm
    blk = pl.BlockSpec((tm, D_MODEL), lambda i: (i, 0))
    vec = pl.BlockSpec((1, D_MODEL), lambda i: (0, 0))
    return pl.pallas_call(
        _combine_kernel,
        grid=(nt,),
        in_specs=[blk, blk, pl.BlockSpec((tm, D_MODEL), lambda i: (nt + i, 0)),
                  pl.BlockSpec((tm, 8), lambda i: (i, 0)), vec, vec],
        out_specs=[blk, blk],
        out_shape=[jax.ShapeDtypeStruct((T, D_MODEL), F32), jax.ShapeDtypeStruct((T, D_MODEL), BF16)],
        compiler_params=_cparams(("arbitrary",)),
        name="combine_ln",
    )(x1, y, y, rt, ln_g.reshape(1, D_MODEL), ln_b.reshape(1, D_MODEL))


def _repack_kernel(sa_ref, sb_ref, cls_ref, a_ref, b_ref, o_ref):
    cls = cls_ref[pl.program_id(0)]
    lane = lax.broadcasted_iota(jnp.int32, a_ref.shape, 1)
    shift = AB_SRC_COL % LANES + 2 * DN_HEADS

    @pl.when(cls == 0)
    def _():
        o_ref[...] = a_ref[...].astype(o_ref.dtype)

    @pl.when(cls == 1)
    def _():
        left = pltpu.roll(a_ref[...], LANES - shift, axis=1)
        right = pltpu.roll(b_ref[...], LANES - shift, axis=1)
        o_ref[...] = jnp.where(lane < LANES - shift, left, right).astype(o_ref.dtype)

    @pl.when(cls == 2)
    def _():
        o_ref[...] = jnp.where(lane < 2 * DN_HEADS, a_ref[...], 0.0).astype(o_ref.dtype)


def repack_w_in(w_in, layer):
    n_aligned = AB_SRC_COL // LANES
    sa = np.array(list(range(n_aligned)) + [n_aligned + j for j in range(COL_AB - n_aligned)] + [n_aligned],
                  np.int32)
    sb = np.minimum(sa + 1, w_in.shape[2] // LANES).astype(np.int32)
    cls = np.array([0] * n_aligned + [1] * (COL_AB - n_aligned) + [2], np.int32)
    rows = w_in.shape[1]
    grid_spec = pltpu.PrefetchScalarGridSpec(
        num_scalar_prefetch=3,
        grid=(NW // LANES,),
        in_specs=[pl.BlockSpec((None, rows, LANES), lambda j, sa, sb, cls: (layer, 0, sa[j])),
                  pl.BlockSpec((None, rows, LANES), lambda j, sa, sb, cls: (layer, 0, sb[j]))],
        out_specs=pl.BlockSpec((rows, LANES), lambda j, sa, sb, cls: (0, j)),
    )
    return pl.pallas_call(
        _repack_kernel,
        grid_spec=grid_spec,
        out_shape=jax.ShapeDtypeStruct((rows, NW), BF16),
        compiler_params=_cparams(("arbitrary",)),
        name="repack_w_in",
    )(jnp.asarray(sa), jnp.asarray(sb), jnp.asarray(cls), w_in, w_in)


def kernel(x, w_in, sgu_norm_g, sgu_norm_b, sgu_w, sgu_b, dn_conv_w, dn_a_log, dn_dt_bias, dn_norm_g,
           diff_lambda_q1, diff_lambda_k1, diff_lambda_q2, diff_lambda_k2, diff_norm_g, w_out, ln1_g, ln1_b,
           router_w, router_bias, moe_w_gate, moe_w_up, moe_w_down, ln2_g, ln2_b):
    B, S, D = x.shape
    T = B * S
    xf = x.reshape(T, D)
    xb = xf.astype(BF16)
    cos_t, sin_t = _rope_tables(S)
    rw_pad = jnp.zeros((D, LANES), F32).at[:, :N_EXPERTS].set(router_w)
    rb_pad = jnp.zeros((1, LANES), F32).at[0, :N_EXPERTS].set(router_bias)
    wg, wu, wd = moe_w_gate.astype(BF16), moe_w_up.astype(BF16), moe_w_down.astype(BF16)
    for l in range(DEPTH):
        h = in_proj(xb, repack_w_in(w_in, l))
        y_a = sgu(h, sgu_norm_g[l], sgu_norm_b[l], sgu_w[l], sgu_b[l])
        y_b = deltanet(h, dn_gates(h, dn_a_log[l], dn_dt_bias[l]), dn_conv_w[l], dn_norm_g[l], B, S)
        qk = rope(h, cos_t, sin_t)
        lam_params = jnp.stack([diff_lambda_q1[l], diff_lambda_k1[l], diff_lambda_q2[l], diff_lambda_k2[l]])
        lambda_init = 0.8 - 0.6 * math.exp(-0.3 * l)
        y_c = diff_attention(qk, vprep(h, B, S), lam_params, diff_norm_g[l], B, S, lambda_init)
        x1, rt = out_proj_ln_router(y_a, y_b, y_c, xf, w_out[l], ln1_g[l], ln1_b[l], rw_pad, rb_pad)
        eidx = rt[:, :TOP_K].astype(jnp.int32)
        y = moe(x1, eidx, wg, wu, wd, l)
        xf, xb = combine_ln(x1, y, rt, ln2_g[l], ln2_b[l])
    return xf.reshape(B, S, D)
```

```python
import functools
import math

import numpy as np
import jax
import jax.numpy as jnp
from jax import lax
from jax.experimental import pallas as pl
from jax.experimental.pallas import tpu as pltpu

F32 = jnp.float32
BF16 = jnp.bfloat16
HIGHEST = lax.Precision.HIGHEST

D_MODEL = 2048
DEPTH = 2
CHUNK = 64
HEAD_DIM = 128
SGU_GROUPS = 4
SGU_WIDTH = SGU_GROUPS * HEAD_DIM
SGU_BLOCK = 128
DN_HEADS = 6
DN_CONV = 4
DIFF_HEADS = 6
DIFF_QK_DIM = 64
ROPE_THETA = 10000.0
N_EXPERTS = 32
N_EXPERT_GROUPS = 4
EXPERTS_PER_GROUP = N_EXPERTS // N_EXPERT_GROUPS
TOP_K = 2
D_EXPERT = D_MODEL // 2
DEEPNORM_ALPHA = (2 * DEPTH) ** 0.25
LN_EPS = 1e-5
RMS_EPS = 1e-6

LANES = 128
VMEM_LIMIT = 56 * 1024 * 1024

HW = DN_HEADS * HEAD_DIM
COL_AU, COL_AV = 0, 4
COL_BQ, COL_BK, COL_BV, COL_BG = 8, 14, 20, 26
COL_CQ, COL_CK, COL_CV = 32, 38, 44
COL_AB = 50
AB_SRC_COL = 2 * SGU_WIDTH + 4 * HW
NW = 51 * LANES
N_TILES_IN = 3

MOE_BLK = 256


def _cparams(sem, vmem=VMEM_LIMIT):
    return pltpu.CompilerParams(dimension_semantics=sem, vmem_limit_bytes=vmem)


HALF_D = D_MODEL // 2


def _pack_rows(x):
    lo = lax.bitcast_convert_type(x[:, :HALF_D].astype(BF16).astype(F32), jnp.uint32)
    hi = lax.bitcast_convert_type(x[:, HALF_D:].astype(BF16).astype(F32), jnp.uint32)
    return (hi & jnp.uint32(0xFFFF0000)) | (lo >> 16)


def _unpack_rows(w):
    lo = lax.bitcast_convert_type(w << 16, F32)
    hi = lax.bitcast_convert_type(w & jnp.uint32(0xFFFF0000), F32)
    return jnp.concatenate([lo, hi], axis=1)


def _matmul_kernel(x_ref, w_ref, o_ref):
    o_ref[...] = jnp.dot(x_ref[...], w_ref[...], preferred_element_type=F32).astype(o_ref.dtype)


def in_proj(xb, w):
    T = xb.shape[0]
    tm = min(512, T)
    tn = NW // N_TILES_IN
    return pl.pallas_call(
        _matmul_kernel,
        grid=(N_TILES_IN, T // tm),
        in_specs=[pl.BlockSpec((tm, D_MODEL), lambda j, i: (i, 0)),
                  pl.BlockSpec((D_MODEL, tn), lambda j, i: (0, j))],
        out_specs=pl.BlockSpec((tm, tn), lambda j, i: (i, j)),
        out_shape=jax.ShapeDtypeStruct((T, NW), BF16),
        compiler_params=_cparams(("arbitrary", "arbitrary")),
        name="in_proj",
    )(xb, w)


def _gelu(t):
    return t * (lax.erf(t * (2.0 ** -0.5)) + 1.0) * 0.5


def _sgu_kernel(u_ref, v_ref, ng_ref, nb_ref, w_ref, b_ref, o_ref, *, rows):
    u = _gelu(u_ref[...].astype(F32))
    v = _gelu(v_ref[...].astype(F32))
    mu = jnp.mean(v, axis=-1, keepdims=True)
    var = jnp.mean(jnp.square(v - mu), axis=-1, keepdims=True)
    vn = ((v - mu) * lax.rsqrt(var + LN_EPS) * ng_ref[...] + nb_ref[...]).astype(BF16)
    ii = lax.broadcasted_iota(jnp.int32, (SGU_BLOCK, SGU_BLOCK), 0)
    jj = lax.broadcasted_iota(jnp.int32, (SGU_BLOCK, SGU_BLOCK), 1)
    w = jnp.where(jj // CHUNK <= ii // CHUNK, w_ref[...], 0.0).astype(BF16)
    bias = b_ref[...]
    for n in range(rows // SGU_BLOCK):
        sl = slice(n * SGU_BLOCK, (n + 1) * SGU_BLOCK)
        s = jnp.dot(w, vn[sl], preferred_element_type=F32) + bias
        o_ref[sl, :] = (u[sl] * s).astype(o_ref.dtype)


def sgu(h, norm_g, norm_b, w_s, b_s):
    T = h.shape[0]
    rows = min(512, T)
    return pl.pallas_call(
        functools.partial(_sgu_kernel, rows=rows),
        grid=(T // rows, SGU_GROUPS),
        in_specs=[pl.BlockSpec((rows, LANES), lambda i, g: (i, COL_AU + g)),
                  pl.BlockSpec((rows, LANES), lambda i, g: (i, COL_AV + g)),
                  pl.BlockSpec((1, LANES), lambda i, g: (0, g)),
                  pl.BlockSpec((1, LANES), lambda i, g: (0, g)),
                  pl.BlockSpec((None, SGU_BLOCK, SGU_BLOCK), lambda i, g: (g, 0, 0)),
                  pl.BlockSpec((None, SGU_BLOCK, 1), lambda i, g: (g, 0, 0))],
        out_specs=pl.BlockSpec((rows, LANES), lambda i, g: (i, g)),
        out_shape=jax.ShapeDtypeStruct((T, SGU_WIDTH), BF16),
        compiler_params=_cparams(("arbitrary", "arbitrary")),
        name="sgu",
    )(h, h, norm_g.reshape(1, SGU_WIDTH), norm_b.reshape(1, SGU_WIDTH), w_s,
      b_s.reshape(SGU_GROUPS, SGU_BLOCK, 1))


def _rope_kernel(x_ref, cos_ref, sin_ref, o_ref):
    half = DIFF_QK_DIM // 2
    scale = jnp.where(pl.program_id(1) < DIFF_HEADS // ROPE_HEADS, DIFF_QK_DIM ** -0.5 * math.log2(math.e), 1.0)
    cos = cos_ref[...]
    sin = sin_ref[...]
    lane = lax.broadcasted_iota(jnp.int32, cos.shape, 1)
    for hh in range(ROPE_HEADS):
        hl = slice(hh * LANES, (hh + 1) * LANES)
        x = x_ref[:, hl].astype(F32)
        partner = jnp.where(lane % DIFF_QK_DIM < half,
                            pltpu.roll(x, LANES - half, axis=1), pltpu.roll(x, half, axis=1))
        o_ref[:, hl] = ((x * cos + partner * sin) * scale).astype(o_ref.dtype)


def _rope_tables(S):
    half = DIFF_QK_DIM // 2
    inv_freq = ROPE_THETA ** (-jnp.arange(half, dtype=F32) / half)
    ang = jnp.arange(S, dtype=F32)[:, None] * inv_freq[None, :]
    cos, sin = jnp.cos(ang), jnp.sin(ang)
    cos_t = jnp.tile(cos, (1, LANES // half))
    sin_t = jnp.tile(jnp.concatenate([-sin, sin], axis=1), (1, LANES // DIFF_QK_DIM))
    return cos_t, sin_t


ROPE_HEADS = 2
ROPE_ROWS = 1024


def rope(h, cos_t, sin_t):
    T = h.shape[0]
    S = cos_t.shape[0]
    rows = min(ROPE_ROWS, S)
    ns = S // rows
    width = ROPE_HEADS * LANES
    return pl.pallas_call(
        _rope_kernel,
        grid=(T // rows, 2 * DIFF_HEADS // ROPE_HEADS),
        in_specs=[pl.BlockSpec((rows, width), lambda i, j: (i, COL_CQ // ROPE_HEADS + j)),
                  pl.BlockSpec((rows, LANES), lambda i, j: (i % ns, 0)),
                  pl.BlockSpec((rows, LANES), lambda i, j: (i % ns, 0))],
        out_specs=pl.BlockSpec((rows, width), lambda i, j: (i, j)),
        out_shape=jax.ShapeDtypeStruct((T, 2 * HW), BF16),
        compiler_params=_cparams(("arbitrary", "arbitrary")),
        name="rope",
    )(h, cos_t, sin_t)


ATT_TILE = 1024
VT_ROWS = HEAD_DIM + 16


def _vprep_kernel(v_ref, o_ref):
    vt = v_ref[...].astype(F32).T
    o_ref[0:HEAD_DIM, :] = vt.astype(o_ref.dtype)
    row = lax.broadcasted_iota(jnp.int32, (VT_ROWS - HEAD_DIM, vt.shape[1]), 0)
    o_ref[HEAD_DIM:VT_ROWS, :] = jnp.where(row == 0, 1.0, 0.0).astype(o_ref.dtype)


def vprep(h, B, S):
    tk = min(ATT_TILE, S)
    nk = S // tk
    return pl.pallas_call(
        _vprep_kernel,
        grid=(B, DIFF_HEADS, nk),
        in_specs=[pl.BlockSpec((tk, LANES), lambda b, hd, j: (b * nk + j, COL_CV + hd))],
        out_specs=pl.BlockSpec((None, None, VT_ROWS, tk), lambda b, hd, j: (b * DIFF_HEADS + hd, j, 0, 0)),
        out_shape=jax.ShapeDtypeStruct((B * DIFF_HEADS, nk, VT_ROWS, tk), BF16),
        compiler_params=_cparams(("arbitrary", "arbitrary", "arbitrary")),
        name="vprep",
    )(h)


def _attn_kernel(q_ref, k_ref, vt_ref, lam_ref, g_ref, o_ref, acc_ref, *, tq, lambda_init):
    i = pl.program_id(2)
    q = q_ref[...]
    lane = lax.broadcasted_iota(jnp.int32, q.shape, 1)
    zero = jnp.zeros_like(q)
    qs = (jnp.where(lane < DIFF_QK_DIM, q, zero), jnp.where(lane >= DIFF_QK_DIM, q, zero))
    acc_ref[...] = jnp.zeros(acc_ref.shape, F32)
    neg = jnp.full((1, tq), -jnp.inf, F32)

    def block(j, ms, masked):
        kb = k_ref[pl.ds(pl.multiple_of(j * tq, tq), tq), :]
        vb = vt_ref[j]
        s = [lax.dot_general(kb, qs[t], (((1,), (1,)), ((), ())), preferred_element_type=F32)
             for t in range(2)]
        if masked:
            kr = lax.broadcasted_iota(jnp.int32, (tq, tq), 0)
            qc = lax.broadcasted_iota(jnp.int32, (tq, tq), 1)
            allowed = kr // CHUNK <= qc // CHUNK
        new = []
        for t in range(2):
            st = jnp.where(allowed, s[t], -jnp.inf) if masked else s[t]
            m_new = jnp.maximum(ms[t], jnp.max(st, axis=0, keepdims=True))
            alpha = jnp.exp2(ms[t] - m_new)
            p = jnp.exp2(st - m_new).astype(BF16)
            acc_ref[t] = alpha * acc_ref[t] + jnp.dot(vb, p, preferred_element_type=F32)
            new.append(m_new)
        return tuple(new)

    ms = lax.fori_loop(0, i, lambda j, ms: block(j, ms, False), (neg, neg))
    block(i, ms, True)

    lam_v = lam_ref[...]
    lam = (jnp.exp(jnp.sum(lam_v[0:1] * lam_v[1:2])) - jnp.exp(jnp.sum(lam_v[2:3] * lam_v[3:4]))
           + lambda_init)
    a0 = acc_ref[0]
    a1 = acc_ref[1]
    o = a0[:HEAD_DIM] / a0[HEAD_DIM:HEAD_DIM + 1] - lam * (a1[:HEAD_DIM] / a1[HEAD_DIM:HEAD_DIM + 1])
    y = o * lax.rsqrt(jnp.mean(o * o, axis=0, keepdims=True) + RMS_EPS) * g_ref[...]
    o_ref[...] = (y * (1.0 - lambda_init)).T.astype(o_ref.dtype)


def diff_attention(qk, vt, lam_params, norm_g, B, S, lambda_init):
    T = qk.shape[0]
    tq = min(ATT_TILE, S)
    nq = S // tq
    return pl.pallas_call(
        functools.partial(_attn_kernel, tq=tq, lambda_init=lambda_init),
        grid=(B, DIFF_HEADS, nq),
        in_specs=[pl.BlockSpec((tq, LANES), lambda b, hd, i: (b * nq + i, hd)),
                  pl.BlockSpec((S, LANES), lambda b, hd, i: (b, DIFF_HEADS + hd)),
                  pl.BlockSpec((None, nq, VT_ROWS, tq), lambda b, hd, i: (b * DIFF_HEADS + hd, 0, 0, 0)),
                  pl.BlockSpec((4, DIFF_QK_DIM), lambda b, hd, i: (0, 0)),
                  pl.BlockSpec((HEAD_DIM, 1), lambda b, hd, i: (0, 0))],
        out_specs=pl.BlockSpec((tq, LANES), lambda b, hd, i: (b * nq + i, hd)),
        out_shape=jax.ShapeDtypeStruct((T, HW), BF16),
        scratch_shapes=[pltpu.VMEM((2, VT_ROWS, tq), F32)],
        compiler_params=_cparams(("arbitrary", "arbitrary", "arbitrary")),
        name="diff_attn",
    )(qk, qk, vt, lam_params, norm_g.reshape(HEAD_DIM, 1))


def _silu(t):
    return t * jax.nn.sigmoid(t)


def _softplus(t):
    return jnp.maximum(t, 0.0) + jnp.log1p(jnp.exp(-jnp.abs(t)))


def _dot32(a, b):
    return jnp.dot(a, b, preferred_element_type=F32, precision=HIGHEST)


def _mm(a, b):
    return jnp.dot(a.astype(BF16), b.astype(BF16), preferred_element_type=F32)


def _mm_nt(a, b):
    return lax.dot_general(a.astype(BF16), b.astype(BF16), (((1,), (1,)), ((), ())),
                           preferred_element_type=F32)


DN_ROWS = 512
DN_HB = 2


def _gates_kernel(ab_ref, alog_ref, dt_ref, o_ref, *, rows):
    ab = ab_ref[...].astype(F32)
    g_all = -jnp.exp(alog_ref[...]) * _softplus(ab + dt_ref[...])
    beta_all = jax.nn.sigmoid(ab)
    ri = lax.broadcasted_iota(jnp.int32, (CHUNK, CHUNK), 0)
    ci = lax.broadcasted_iota(jnp.int32, (CHUNK, CHUNK), 1)
    tri = (ri >= ci).astype(F32)
    is_decay_lane = lax.broadcasted_iota(jnp.int32, (CHUNK, LANES), 1) < DN_HEADS
    for c in range(rows // CHUNK):
        sl = slice(c * CHUNK, (c + 1) * CHUNK)
        o_ref[sl, :] = jnp.where(is_decay_lane, _dot32(tri, g_all[sl]), beta_all[sl])


def dn_gates(h, a_log, dt_bias):
    T = h.shape[0]
    rows = min(512, T)
    alog_row = jnp.zeros((1, LANES), F32).at[0, :DN_HEADS].set(a_log)
    dt_row = jnp.zeros((1, LANES), F32).at[0, :DN_HEADS].set(dt_bias)
    row = pl.BlockSpec((1, LANES), lambda i: (0, 0))
    return pl.pallas_call(
        functools.partial(_gates_kernel, rows=rows),
        grid=(T // rows,),
        in_specs=[pl.BlockSpec((rows, LANES), lambda i: (i, COL_AB)), row, row],
        out_specs=pl.BlockSpec((rows, LANES), lambda i: (i, 0)),
        out_shape=jax.ShapeDtypeStruct((T, LANES), F32),
        compiler_params=_cparams(("arbitrary",)),
        name="dn_gates",
    )(h, alog_row, dt_row)


def _dn_kernel(q_ref, k_ref, v_ref, gate_ref, gb_ref, cwq_ref, cwk_ref, cwv_ref, ng_ref, o_ref,
               ext_ref, tail_ref, state_ref, *, rows, hb):
    hd0 = pl.program_id(1) * hb
    r = pl.program_id(2)
    pad = 8
    nc = rows // CHUNK

    @pl.when(r == 0)
    def _():
        tail_ref[...] = jnp.zeros(tail_ref.shape, F32)
        state_ref[...] = jnp.zeros(state_ref.shape, F32)

    def conv_silu(x_ref, cw_ref, slot):
        x = x_ref[...].astype(F32)
        ext_ref[0:pad, :] = tail_ref[slot]
        ext_ref[pad:pad + rows, :] = x
        tail_ref[slot] = x[rows - pad:rows]
        cw = cw_ref[...]
        acc = None
        for t in range(DN_CONV):
            off = pad - (DN_CONV - 1) + t
            term = ext_ref[off:off + rows, :] * cw[t:t + 1, :]
            acc = term if acc is None else acc + term
        return _silu(acc)

    def l2n(t):
        return t * lax.rsqrt(jnp.sum(t * t, axis=-1, keepdims=True) + RMS_EPS)

    qf = conv_silu(q_ref, cwq_ref, 0)
    kf = conv_silu(k_ref, cwk_ref, 1)
    vf = conv_silu(v_ref, cwv_ref, 2)
    gb = gb_ref[...]
    lane = lax.broadcasted_iota(jnp.int32, gb.shape, 1)

    ri = lax.broadcasted_iota(jnp.int32, (CHUNK, CHUNK), 0)
    ci = lax.broadcasted_iota(jnp.int32, (CHUNK, CHUNK), 1)
    causal = ri >= ci
    strict = ri > ci
    eye = (ri == ci).astype(F32)

    units = []
    for hh in range(hb):
        hl = slice(hh * HEAD_DIM, (hh + 1) * HEAD_DIM)
        q = l2n(qf[:, hl]) * (HEAD_DIM ** -0.5)
        k = l2n(kf[:, hl])
        gc_col = jnp.sum(jnp.where(lane == hd0 + hh, gb, 0.0), axis=-1, keepdims=True)
        beta = jnp.sum(jnp.where(lane == hd0 + hh + DN_HEADS, gb, 0.0), axis=-1, keepdims=True)
        gcb = jnp.broadcast_to(gc_col, (rows, HEAD_DIM))
        eg = jnp.exp(gcb)
        kb = k * beta
        rhs = jnp.concatenate([vf[:, hl] * beta, kb * eg], axis=-1)
        qg = q * eg
        for c in range(nc):
            sl = slice(c * CHUNK, (c + 1) * CHUNK)
            units.append(dict(q=q[sl], k=k[sl], kb=kb[sl], rhs=rhs[sl], qg=qg[sl], g=gcb[sl]))
    for un in units:
        g = un["g"]
        gdiff = g[:, :CHUNK] - g.T[:CHUNK, :]
        un["decay"] = jnp.exp(jnp.where(causal, gdiff, -jnp.inf))
    for un in units:
        un["m"] = jnp.where(strict, _mm_nt(un["kb"], un["k"]) * un["decay"], 0.0)
    for un in units:
        un["tinv"] = eye - un["m"]
        un["pw"] = _mm(un["m"], un["m"])
    for it in range(5):
        for un in units:
            un["tinv"] = un["tinv"] + _mm(un["tinv"], un["pw"])
        if it < 4:
            for un in units:
                un["pw"] = _mm(un["pw"], un["pw"])
    for un in units:
        un["sol"] = _mm(un["tinv"], un["rhs"])
        un["attn"] = _mm_nt(un["q"], un["k"]) * un["decay"]

    gate = gate_ref[...].astype(F32)
    ng = ng_ref[...]
    states = [state_ref[hh] for hh in range(hb)]
    for c in range(nc):
        sl = slice(c * CHUNK, (c + 1) * CHUNK)
        for hh in range(hb):
            un = units[hh * nc + c]
            hl = slice(hh * HEAD_DIM, (hh + 1) * HEAD_DIM)
            u, w = un["sol"][:, :HEAD_DIM], un["sol"][:, HEAD_DIM:]
            g = un["g"]
            g_last = g[CHUNK - 1:CHUNK, :]
            sb = states[hh].astype(BF16)
            v_new = u - jnp.dot(w.astype(BF16), sb, preferred_element_type=F32)
            vnb = v_new.astype(BF16)
            o = (jnp.dot(un["qg"].astype(BF16), sb, preferred_element_type=F32)
                 + jnp.dot(un["attn"].astype(BF16), vnb, preferred_element_type=F32))
            k_dec = un["k"] * jnp.exp(g_last - g)
            states[hh] = states[hh] * jnp.exp(g_last) + jnp.dot(k_dec.T.astype(BF16), vnb,
                                                                preferred_element_type=F32)
            y = o * lax.rsqrt(jnp.mean(o * o, axis=-1, keepdims=True) + RMS_EPS) * ng
            o_ref[sl, hl] = (y * _silu(gate[sl, hl])).astype(o_ref.dtype)
    for hh in range(hb):
        state_ref[hh] = states[hh]


def deltanet(h, gb, conv_w, norm_g, B, S):
    T = h.shape[0]
    rows = min(DN_ROWS, S)
    nr = S // rows
    hb = DN_HB
    width = hb * HEAD_DIM

    def col(c):
        return pl.BlockSpec((rows, width), lambda b, hg, r: (b * nr + r, c // hb + hg))

    def cw(c):
        return pl.BlockSpec((DN_CONV, width), lambda b, hg, r: (0, c // hb + hg))

    return pl.pallas_call(
        functools.partial(_dn_kernel, rows=rows, hb=hb),
        grid=(B, DN_HEADS // hb, nr),
        in_specs=[col(COL_BQ), col(COL_BK), col(COL_BV), col(COL_BG),
                  pl.BlockSpec((rows, LANES), lambda b, hg, r: (b * nr + r, 0)),
                  cw(0), cw(DN_HEADS), cw(2 * DN_HEADS),
                  pl.BlockSpec((1, LANES), lambda b, hg, r: (0, 0))],
        out_specs=pl.BlockSpec((rows, width), lambda b, hg, r: (b * nr + r, hg)),
        out_shape=jax.ShapeDtypeStruct((T, HW), BF16),
        scratch_shapes=[pltpu.VMEM((rows + 8, width), F32), pltpu.VMEM((3, 8, width), F32),
                        pltpu.VMEM((hb, HEAD_DIM, HEAD_DIM), F32)],
        compiler_params=_cparams(("arbitrary", "arbitrary", "arbitrary")),
        name="deltanet",
    )(h, h, h, h, gb, conv_w, conv_w, conv_w, norm_g.reshape(1, LANES))


def _layer_norm(z, g, b):
    mu = jnp.mean(z, axis=-1, keepdims=True)
    var = jnp.mean(jnp.square(z - mu), axis=-1, keepdims=True)
    return (z - mu) * lax.rsqrt(var + LN_EPS) * g + b


def _outproj_kernel(ya_ref, yb_ref, yc_ref, x_ref, wa_ref, wb_ref, wc_ref, g_ref, b_ref, rwh_ref, rwl_ref,
                    rb_ref, x1_ref, xp_ref, rt_ref):
    mixed = (jnp.dot(ya_ref[...], wa_ref[...], preferred_element_type=F32)
             + jnp.dot(yb_ref[...], wb_ref[...], preferred_element_type=F32)
             + jnp.dot(yc_ref[...], wc_ref[...], preferred_element_type=F32))
    x1 = _layer_norm(DEEPNORM_ALPHA * x_ref[...] + mixed, g_ref[...], b_ref[...])
    x1_ref[...] = x1
    xp_ref[...] = _pack_rows(x1)

    x_hi = x1.astype(BF16)
    x_lo = (x1 - x_hi.astype(F32)).astype(BF16)
    logits = (jnp.dot(x_hi, rwh_ref[...], preferred_element_type=F32)
              + jnp.dot(x_lo, rwh_ref[...], preferred_element_type=F32)
              + jnp.dot(x_hi, rwl_ref[...], preferred_element_type=F32))
    scores = jax.nn.sigmoid(logits)
    tm = scores.shape[0]
    lane = lax.broadcasted_iota(jnp.int32, scores.shape, 1)
    lanef = lane.astype(F32)
    neg = -jnp.inf
    big = float(LANES)
    sel = jnp.where(lane < N_EXPERTS, scores + rb_ref[...], neg)

    def top2(mg):
        t1 = jnp.max(mg, axis=-1, keepdims=True)
        i1 = jnp.min(jnp.where(mg == t1, lanef, big), axis=-1, keepdims=True)
        mg2 = jnp.where(lanef == i1, neg, mg)
        t2 = jnp.max(mg2, axis=-1, keepdims=True)
        i2 = jnp.min(jnp.where(mg2 == t2, lanef, big), axis=-1, keepdims=True)
        return t1 + t2, i1, i2

    best_s = best_1 = best_2 = None
    for gidx in range(N_EXPERT_GROUPS):
        in_g = (lane >= gidx * EXPERTS_PER_GROUP) & (lane < (gidx + 1) * EXPERTS_PER_GROUP)
        gs, i1, i2 = top2(jnp.where(in_g, sel, neg))
        if best_s is None:
            best_s, best_1, best_2 = gs, i1, i2
        else:
            take = gs > best_s
            best_s = jnp.where(take, gs, best_s)
            best_1 = jnp.where(take, i1, best_1)
            best_2 = jnp.where(take, i2, best_2)
    s1 = jnp.sum(jnp.where(lanef == best_1, scores, 0.0), axis=-1, keepdims=True)
    s2 = jnp.sum(jnp.where(lanef == best_2, scores, 0.0), axis=-1, keepdims=True)
    tot = s1 + s2
    l8 = lax.broadcasted_iota(jnp.int32, (tm, 8), 1)
    rt_ref[...] = jnp.where(l8 == 0, best_1, jnp.where(l8 == 1, best_2, jnp.where(
        l8 == 2, s1 / tot, jnp.where(l8 == 3, s2 / tot, 0.0))))


def out_proj_ln_router(ya, yb, yc, x, w_out, ln_g, ln_b, rw_pad, rb_pad):
    T = x.shape[0]
    tm = min(256, T)
    wa = w_out[:SGU_WIDTH].astype(BF16)
    wb = w_out[SGU_WIDTH:SGU_WIDTH + HW].astype(BF16)
    wc = w_out[SGU_WIDTH + HW:].astype(BF16)
    rw_hi = rw_pad.astype(BF16)
    rw_lo = (rw_pad - rw_hi.astype(F32)).astype(BF16)

    def rowblk(width):
        return pl.BlockSpec((tm, width), lambda i: (i, 0))

    def full(shape):
        return pl.BlockSpec(shape, lambda i: (0, 0))

    return pl.pallas_call(
        _outproj_kernel,
        grid=(T // tm,),
        in_specs=[rowblk(SGU_WIDTH), rowblk(HW), rowblk(HW), rowblk(D_MODEL),
                  full((SGU_WIDTH, D_MODEL)), full((HW, D_MODEL)), full((HW, D_MODEL)),
                  full((1, D_MODEL)), full((1, D_MODEL)), full((D_MODEL, LANES)), full((D_MODEL, LANES)),
                  full((1, LANES))],
        out_specs=[rowblk(D_MODEL), rowblk(HALF_D), rowblk(8)],
        out_shape=[jax.ShapeDtypeStruct((T, D_MODEL), F32), jax.ShapeDtypeStruct((T, HALF_D), jnp.uint32),
                   jax.ShapeDtypeStruct((T, 8), F32)],
        compiler_params=_cparams(("arbitrary",)),
        name="out_proj_ln_router",
    )(ya, yb, yc, x, wa, wb, wc, ln_g.reshape(1, D_MODEL), ln_b.reshape(1, D_MODEL), rw_hi, rw_lo, rb_pad)


def _route_tables(eidx, T, nb):
    A = T * TOP_K
    flat_e = eidx.reshape(A)
    order = jnp.argsort(flat_e, stable=True).astype(jnp.int32)
    experts = jnp.arange(N_EXPERTS, dtype=jnp.int32)
    counts = jnp.sum((flat_e[:, None] == experts[None, :]).astype(jnp.int32), axis=0)
    padded = (counts + MOE_BLK - 1) // MOE_BLK * MOE_BLK
    pad_end = jnp.cumsum(padded)
    pad_start = pad_end - padded
    start = jnp.cumsum(counts) - counts
    pos = jnp.arange(nb * MOE_BLK, dtype=jnp.int32)
    e_of_pos = jnp.sum((pos[:, None] >= pad_end[None, :]).astype(jnp.int32), axis=1)
    onehot = e_of_pos[:, None] == experts[None, :]

    def lookup(tab):
        return jnp.sum(jnp.where(onehot, tab[None, :], 0), axis=1)

    rank = pos - lookup(pad_start)
    valid = rank < lookup(counts)
    a_sorted = order[jnp.clip(lookup(start) + rank, 0, A - 1)]
    tok = a_sorted // TOP_K
    slot = a_sorted % TOP_K
    src = jnp.where(valid, tok, 0)
    dst = jnp.where(valid, slot * T + tok, 0)
    block_expert = jnp.minimum(e_of_pos.reshape(nb, MOE_BLK)[:, 0], N_EXPERTS - 1)
    n_valid = jnp.sum(valid.reshape(nb, MOE_BLK).astype(jnp.int32), axis=1)
    return block_expert, n_valid, src.reshape(nb, 1, MOE_BLK), dst.reshape(nb, 1, MOE_BLK)


def _moe_kernel(be_ref, nv_ref, src_ref, srcn_ref, dst_ref, dstp_ref, x_hbm, wg_ref, wu_ref, wd_ref, y_hbm,
                xbuf, ybuf, gsem, ssem):
    b = pl.program_id(0)
    slot = b % 2
    nv_cur = nv_ref[b]
    nv_next = nv_ref[b + 1]
    nv_prev2 = nv_ref[jnp.maximum(b - 2, 0)]

    def row_gather(idx_ref, s, r):
        return pltpu.make_async_copy(x_hbm.at[pl.ds(idx_ref[0, r], 1), :], xbuf.at[s, pl.ds(r, 1), :],
                                     gsem.at[s])

    def row_scatter(idx_ref, s, r):
        return pltpu.make_async_copy(ybuf.at[s, pl.ds(r, 1), :], y_hbm.at[pl.ds(idx_ref[0, r], 1), :],
                                     ssem.at[s])

    def for_rows(n, fn):
        def body(r, carry):
            fn(r)
            return carry
        if isinstance(n, int):
            lax.fori_loop(0, n, body, 0, unroll=8)
        else:
            lax.fori_loop(0, n, body, 0)

    @pl.when(jnp.logical_and(b == 0, nv_cur > 0))
    def _():
        for_rows(MOE_BLK, lambda r: row_gather(src_ref, 0, r).start())

    @pl.when(nv_next > 0)
    def _():
        for_rows(MOE_BLK, lambda r: row_gather(srcn_ref, 1 - slot, r).start())

    @pl.when(jnp.logical_and(b >= 2, nv_prev2 > 0))
    def _():
        for_rows(nv_prev2, lambda r: row_scatter(dstp_ref, slot, r).wait())

    @pl.when(nv_cur > 0)
    def _():
        for_rows(MOE_BLK, lambda r: row_gather(src_ref, slot, r).wait())
        xb = _unpack_rows(xbuf[slot]).astype(BF16)
        hg = jnp.dot(xb, wg_ref[...], preferred_element_type=F32)
        hu = jnp.dot(xb, wu_ref[...], preferred_element_type=F32)
        hb = (_silu(hg) * hu).astype(BF16)
        ybuf[slot] = _pack_rows(jnp.dot(hb, wd_ref[...], preferred_element_type=F32))
        for_rows(nv_cur, lambda r: row_scatter(dst_ref, slot, r).start())


def moe(xp, eidx, wg, wu, wd, layer):
    T = xp.shape[0]
    A = T * TOP_K
    nb = -(-A // MOE_BLK) + N_EXPERTS
    drain = 2
    block_expert, n_valid, src, dst = _route_tables(eidx, T, nb)
    block_expert = jnp.concatenate([block_expert, jnp.broadcast_to(block_expert[-1:], (drain,))])
    n_valid = jnp.concatenate([n_valid, jnp.zeros((drain + 1,), jnp.int32)])

    def idx_blk(shift):
        return pl.BlockSpec((None, 1, MOE_BLK), lambda b, be, nv: (jnp.clip(b + shift, 0, nb - 1), 0, 0),
                            memory_space=pltpu.SMEM)

    def w_blk(rows, cols):
        return pl.BlockSpec((None, None, rows, cols), lambda b, be, nv: (layer, be[b], 0, 0))

    grid_spec = pltpu.PrefetchScalarGridSpec(
        num_scalar_prefetch=2,
        grid=(nb + drain,),
        in_specs=[idx_blk(0), idx_blk(1), idx_blk(0), idx_blk(-2),
                  pl.BlockSpec(memory_space=pl.ANY),
                  w_blk(D_MODEL, D_EXPERT), w_blk(D_MODEL, D_EXPERT), w_blk(D_EXPERT, D_MODEL)],
        out_specs=pl.BlockSpec(memory_space=pl.ANY),
        scratch_shapes=[pltpu.VMEM((2, MOE_BLK, HALF_D), jnp.uint32), pltpu.VMEM((2, MOE_BLK, HALF_D), jnp.uint32),
                        pltpu.SemaphoreType.DMA((2,)), pltpu.SemaphoreType.DMA((2,))],
    )
    return pl.pallas_call(
        _moe_kernel,
        grid_spec=grid_spec,
        out_shape=jax.ShapeDtypeStruct((A, HALF_D), jnp.uint32),
        compiler_params=_cparams(("arbitrary",)),
        name="moe",
    )(block_expert, n_valid, src, src, dst, dst, xp, wg, wu, wd)


def _combine_kernel(x_ref, y0_ref, y1_ref, rt_ref, g_ref, b_ref, o_ref, ob_ref):
    rt = rt_ref[...]
    ffn = _unpack_rows(y0_ref[...]) * rt[:, 2:3] + _unpack_rows(y1_ref[...]) * rt[:, 3:4]
    x2 = _layer_norm(DEEPNORM_ALPHA * x_ref[...] + ffn, g_ref[...], b_ref[...])
    o_ref[...] = x2
    ob_ref[...] = x2.astype(BF16)


def combine_ln(x1, y, rt, ln_g, ln_b):
    T = x1.shape[0]
    tm = min(256, T)
    nt = T // tm
    blk = pl.BlockSpec((tm, D_MODEL), lambda i: (i, 0))
    vec = pl.BlockSpec((1, D_MODEL), lambda i: (0, 0))
    return pl.pallas_call(
        _combine_kernel,
        grid=(nt,),
        in_specs=[blk, pl.BlockSpec((tm, HALF_D), lambda i: (i, 0)), pl.BlockSpec((tm, HALF_D), lambda i: (nt + i, 0)),
                  pl.BlockSpec((tm, 8), lambda i: (i, 0)), vec, vec],
        out_specs=[blk, blk],
        out_shape=[jax.ShapeDtypeStruct((T, D_MODEL), F32), jax.ShapeDtypeStruct((T, D_MODEL), BF16)],
        compiler_params=_cparams(("arbitrary",)),
        name="combine_ln",
    )(x1, y, y, rt, ln_g.reshape(1, D_MODEL), ln_b.reshape(1, D_MODEL))


def _repack_kernel(sa_ref, sb_ref, cls_ref, a_ref, b_ref, o_ref):
    cls = cls_ref[pl.program_id(0)]
    lane = lax.broadcasted_iota(jnp.int32, a_ref.shape, 1)
    shift = AB_SRC_COL % LANES + 2 * DN_HEADS

    @pl.when(cls == 0)
    def _():
        o_ref[...] = a_ref[...].astype(o_ref.dtype)

    @pl.when(cls == 1)
    def _():
        left = pltpu.roll(a_ref[...], LANES - shift, axis=1)
        right = pltpu.roll(b_ref[...], LANES - shift, axis=1)
        o_ref[...] = jnp.where(lane < LANES - shift, left, right).astype(o_ref.dtype)

    @pl.when(cls == 2)
    def _():
        o_ref[...] = jnp.where(lane < 2 * DN_HEADS, a_ref[...], 0.0).astype(o_ref.dtype)


def repack_w_in(w_in, layer):
    n_aligned = AB_SRC_COL // LANES
    sa = np.array(list(range(n_aligned)) + [n_aligned + j for j in range(COL_AB - n_aligned)] + [n_aligned],
                  np.int32)
    sb = np.minimum(sa + 1, w_in.shape[2] // LANES).astype(np.int32)
    cls = np.array([0] * n_aligned + [1] * (COL_AB - n_aligned) + [2], np.int32)
    rows = w_in.shape[1]
    grid_spec = pltpu.PrefetchScalarGridSpec(
        num_scalar_prefetch=3,
        grid=(NW // LANES,),
        in_specs=[pl.BlockSpec((None, rows, LANES), lambda j, sa, sb, cls: (layer, 0, sa[j])),
                  pl.BlockSpec((None, rows, LANES), lambda j, sa, sb, cls: (layer, 0, sb[j]))],
        out_specs=pl.BlockSpec((rows, LANES), lambda j, sa, sb, cls: (0, j)),
    )
    return pl.pallas_call(
        _repack_kernel,
        grid_spec=grid_spec,
        out_shape=jax.ShapeDtypeStruct((rows, NW), BF16),
        compiler_params=_cparams(("arbitrary",)),
        name="repack_w_in",
    )(jnp.asarray(sa), jnp.asarray(sb), jnp.asarray(cls), w_in, w_in)


def kernel(x, w_in, sgu_norm_g, sgu_norm_b, sgu_w, sgu_b, dn_conv_w, dn_a_log, dn_dt_bias, dn_norm_g,
           diff_lambda_q1, diff_lambda_k1, diff_lambda_q2, diff_lambda_k2, diff_norm_g, w_out, ln1_g, ln1_b,
           router_w, router_bias, moe_w_gate, moe_w_up, moe_w_down, ln2_g, ln2_b):
    B, S, D = x.shape
    T = B * S
    xf = x.reshape(T, D)
    xb = xf.astype(BF16)
    cos_t, sin_t = _rope_tables(S)
    rw_pad = jnp.zeros((D, LANES), F32).at[:, :N_EXPERTS].set(router_w)
    rb_pad = jnp.zeros((1, LANES), F32).at[0, :N_EXPERTS].set(router_bias)
    wg, wu, wd = moe_w_gate.astype(BF16), moe_w_up.astype(BF16), moe_w_down.astype(BF16)
    for l in range(DEPTH):
        h = in_proj(xb, repack_w_in(w_in, l))
        y_a = sgu(h, sgu_norm_g[l], sgu_norm_b[l], sgu_w[l], sgu_b[l])
        y_b = deltanet(h, dn_gates(h, dn_a_log[l], dn_dt_bias[l]), dn_conv_w[l], dn_norm_g[l], B, S)
        qk = rope(h, cos_t, sin_t)
        lam_params = jnp.stack([diff_lambda_q1[l], diff_lambda_k1[l], diff_lambda_q2[l], diff_lambda_k2[l]])
        lambda_init = 0.8 - 0.6 * math.exp(-0.3 * l)
        y_c = diff_attention(qk, vprep(h, B, S), lam_params, diff_norm_g[l], B, S, lambda_init)
        x1, xp, rt = out_proj_ln_router(y_a, y_b, y_c, xf, w_out[l], ln1_g[l], ln1_b[l], rw_pad, rb_pad)
        eidx = rt[:, :TOP_K].astype(jnp.int32)
        y = moe(xp, eidx, wg, wu, wd, l)
        xf, xb = combine_ln(x1, y, rt, ln2_g[l], ln2_b[l])
    return xf.reshape(B, S, D)
```

```python
import functools
import math

import numpy as np
import jax
import jax.numpy as jnp
from jax import lax
from jax.experimental import pallas as pl
from jax.experimental.pallas import tpu as pltpu
from jax.experimental.pallas import tpu_sc as plsc

F32 = jnp.float32
BF16 = jnp.bfloat16
HIGHEST = lax.Precision.HIGHEST

D_MODEL = 2048
DEPTH = 2
CHUNK = 64
HEAD_DIM = 128
SGU_GROUPS = 4
SGU_WIDTH = SGU_GROUPS * HEAD_DIM
SGU_BLOCK = 128
DN_HEADS = 6
DN_CONV = 4
DIFF_HEADS = 6
DIFF_QK_DIM = 64
ROPE_THETA = 10000.0
N_EXPERTS = 32
N_EXPERT_GROUPS = 4
EXPERTS_PER_GROUP = N_EXPERTS // N_EXPERT_GROUPS
TOP_K = 2
D_EXPERT = D_MODEL // 2
DEEPNORM_ALPHA = (2 * DEPTH) ** 0.25
LN_EPS = 1e-5
RMS_EPS = 1e-6

LANES = 128
VMEM_LIMIT = 56 * 1024 * 1024

HW = DN_HEADS * HEAD_DIM
COL_AU, COL_AV = 0, 4
COL_BQ, COL_BK, COL_BV, COL_BG = 8, 14, 20, 26
COL_CQ, COL_CK, COL_CV = 32, 38, 44
COL_AB = 50
AB_SRC_COL = 2 * SGU_WIDTH + 4 * HW
NW = 51 * LANES
N_TILES_IN = 3

MOE_BLK = 256


def _cparams(sem, vmem=VMEM_LIMIT):
    return pltpu.CompilerParams(dimension_semantics=sem, vmem_limit_bytes=vmem)


HALF_D = D_MODEL // 2


def _pack_rows(x):
    lo = lax.bitcast_convert_type(x[:, :HALF_D].astype(BF16).astype(F32), jnp.uint32)
    hi = lax.bitcast_convert_type(x[:, HALF_D:].astype(BF16).astype(F32), jnp.uint32)
    return (hi & jnp.uint32(0xFFFF0000)) | (lo >> 16)


def _unpack_rows(w):
    lo = lax.bitcast_convert_type(w << 16, F32)
    hi = lax.bitcast_convert_type(w & jnp.uint32(0xFFFF0000), F32)
    return jnp.concatenate([lo, hi], axis=1)


N_PLANES = 4
PLANE_W = HALF_D // N_PLANES


def _store_planes(ref, packed):
    for j in range(N_PLANES):
        ref[j] = packed[:, j * PLANE_W:(j + 1) * PLANE_W]


def _load_planes(ref):
    return jnp.concatenate([ref[j] for j in range(N_PLANES)], axis=1)


SC_WINDOW = 128


def sc_gather(x, idx):
    M = idx.shape[0]
    W = x.shape[1]
    mesh = plsc.VectorSubcoreMesh(core_axis_name="core", subcore_axis_name="subcore")

    @functools.partial(pl.kernel, out_type=jax.ShapeDtypeStruct((M, W), x.dtype), mesh=mesh, scratch_types=[])
    def gather_kernel(x_hbm, i_hbm, o_hbm):
        def body(i_vmem, o_vmem):
            pltpu.sync_copy(x_hbm.at[i_vmem.at[0]], o_vmem)

        pltpu.emit_pipeline(
            body,
            grid=(M // SC_WINDOW,),
            in_specs=[pl.BlockSpec((1, SC_WINDOW), index_map=lambda i: (0, i))],
            out_specs=[pl.BlockSpec((SC_WINDOW, W), index_map=lambda i: (i, 0))],
            core_axis_name=("core", "subcore"),
            dimension_semantics=(pltpu.PARALLEL,),
        )(i_hbm, o_hbm)

    return gather_kernel(x, idx.reshape(1, M))


def _gather_planes(x, idx):
    n = x.shape[1]
    flat_idx = (jnp.arange(N_PLANES, dtype=jnp.int32)[:, None] * n + idx[None, :]).reshape(-1)
    return sc_gather(x.reshape(N_PLANES * n, PLANE_W), flat_idx).reshape(N_PLANES, idx.shape[0], PLANE_W)


def _matmul_kernel(x_ref, w_ref, o_ref):
    o_ref[...] = jnp.dot(x_ref[...], w_ref[...], preferred_element_type=F32).astype(o_ref.dtype)


def in_proj(xb, w):
    T = xb.shape[0]
    tm = min(512, T)
    tn = NW // N_TILES_IN
    return pl.pallas_call(
        _matmul_kernel,
        grid=(N_TILES_IN, T // tm),
        in_specs=[pl.BlockSpec((tm, D_MODEL), lambda j, i: (i, 0)),
                  pl.BlockSpec((D_MODEL, tn), lambda j, i: (0, j))],
        out_specs=pl.BlockSpec((tm, tn), lambda j, i: (i, j)),
        out_shape=jax.ShapeDtypeStruct((T, NW), BF16),
        compiler_params=_cparams(("arbitrary", "arbitrary")),
        name="in_proj",
    )(xb, w)


def _gelu(t):
    return t * (lax.erf(t * (2.0 ** -0.5)) + 1.0) * 0.5


def _sgu_kernel(u_ref, v_ref, ng_ref, nb_ref, w_ref, b_ref, o_ref, *, rows):
    u = _gelu(u_ref[...].astype(F32))
    v = _gelu(v_ref[...].astype(F32))
    mu = jnp.mean(v, axis=-1, keepdims=True)
    var = jnp.mean(jnp.square(v - mu), axis=-1, keepdims=True)
    vn = ((v - mu) * lax.rsqrt(var + LN_EPS) * ng_ref[...] + nb_ref[...]).astype(BF16)
    ii = lax.broadcasted_iota(jnp.int32, (SGU_BLOCK, SGU_BLOCK), 0)
    jj = lax.broadcasted_iota(jnp.int32, (SGU_BLOCK, SGU_BLOCK), 1)
    w = jnp.where(jj // CHUNK <= ii // CHUNK, w_ref[...], 0.0).astype(BF16)
    bias = b_ref[...]
    for n in range(rows // SGU_BLOCK):
        sl = slice(n * SGU_BLOCK, (n + 1) * SGU_BLOCK)
        s = jnp.dot(w, vn[sl], preferred_element_type=F32) + bias
        o_ref[sl, :] = (u[sl] * s).astype(o_ref.dtype)


def sgu(h, norm_g, norm_b, w_s, b_s):
    T = h.shape[0]
    rows = min(512, T)
    return pl.pallas_call(
        functools.partial(_sgu_kernel, rows=rows),
        grid=(T // rows, SGU_GROUPS),
        in_specs=[pl.BlockSpec((rows, LANES), lambda i, g: (i, COL_AU + g)),
                  pl.BlockSpec((rows, LANES), lambda i, g: (i, COL_AV + g)),
                  pl.BlockSpec((1, LANES), lambda i, g: (0, g)),
                  pl.BlockSpec((1, LANES), lambda i, g: (0, g)),
                  pl.BlockSpec((None, SGU_BLOCK, SGU_BLOCK), lambda i, g: (g, 0, 0)),
                  pl.BlockSpec((None, SGU_BLOCK, 1), lambda i, g: (g, 0, 0))],
        out_specs=pl.BlockSpec((rows, LANES), lambda i, g: (i, g)),
        out_shape=jax.ShapeDtypeStruct((T, SGU_WIDTH), BF16),
        compiler_params=_cparams(("arbitrary", "arbitrary")),
        name="sgu",
    )(h, h, norm_g.reshape(1, SGU_WIDTH), norm_b.reshape(1, SGU_WIDTH), w_s,
      b_s.reshape(SGU_GROUPS, SGU_BLOCK, 1))


def _rope_kernel(x_ref, cos_ref, sin_ref, o_ref):
    half = DIFF_QK_DIM // 2
    scale = jnp.where(pl.program_id(1) < DIFF_HEADS // ROPE_HEADS, DIFF_QK_DIM ** -0.5 * math.log2(math.e), 1.0)
    cos = cos_ref[...]
    sin = sin_ref[...]
    lane = lax.broadcasted_iota(jnp.int32, cos.shape, 1)
    for hh in range(ROPE_HEADS):
        hl = slice(hh * LANES, (hh + 1) * LANES)
        x = x_ref[:, hl].astype(F32)
        partner = jnp.where(lane % DIFF_QK_DIM < half,
                            pltpu.roll(x, LANES - half, axis=1), pltpu.roll(x, half, axis=1))
        o_ref[:, hl] = ((x * cos + partner * sin) * scale).astype(o_ref.dtype)


def _rope_tables(S):
    half = DIFF_QK_DIM // 2
    inv_freq = ROPE_THETA ** (-jnp.arange(half, dtype=F32) / half)
    ang = jnp.arange(S, dtype=F32)[:, None] * inv_freq[None, :]
    cos, sin = jnp.cos(ang), jnp.sin(ang)
    cos_t = jnp.tile(cos, (1, LANES // half))
    sin_t = jnp.tile(jnp.concatenate([-sin, sin], axis=1), (1, LANES // DIFF_QK_DIM))
    return cos_t, sin_t


ROPE_HEADS = 2
ROPE_ROWS = 1024


def rope(h, cos_t, sin_t):
    T = h.shape[0]
    S = cos_t.shape[0]
    rows = min(ROPE_ROWS, S)
    ns = S // rows
    width = ROPE_HEADS * LANES
    return pl.pallas_call(
        _rope_kernel,
        grid=(T // rows, 2 * DIFF_HEADS // ROPE_HEADS),
        in_specs=[pl.BlockSpec((rows, width), lambda i, j: (i, COL_CQ // ROPE_HEADS + j)),
                  pl.BlockSpec((rows, LANES), lambda i, j: (i % ns, 0)),
                  pl.BlockSpec((rows, LANES), lambda i, j: (i % ns, 0))],
        out_specs=pl.BlockSpec((rows, width), lambda i, j: (i, j)),
        out_shape=jax.ShapeDtypeStruct((T, 2 * HW), BF16),
        compiler_params=_cparams(("arbitrary", "arbitrary")),
        name="rope",
    )(h, cos_t, sin_t)


ATT_TILE = 1024
VT_ROWS = HEAD_DIM + 16


def _vprep_kernel(v_ref, o_ref):
    vt = v_ref[...].astype(F32).T
    o_ref[0:HEAD_DIM, :] = vt.astype(o_ref.dtype)
    row = lax.broadcasted_iota(jnp.int32, (VT_ROWS - HEAD_DIM, vt.shape[1]), 0)
    o_ref[HEAD_DIM:VT_ROWS, :] = jnp.where(row == 0, 1.0, 0.0).astype(o_ref.dtype)


def vprep(h, B, S):
    tk = min(ATT_TILE, S)
    nk = S // tk
    return pl.pallas_call(
        _vprep_kernel,
        grid=(B, DIFF_HEADS, nk),
        in_specs=[pl.BlockSpec((tk, LANES), lambda b, hd, j: (b * nk + j, COL_CV + hd))],
        out_specs=pl.BlockSpec((None, None, VT_ROWS, tk), lambda b, hd, j: (b * DIFF_HEADS + hd, j, 0, 0)),
        out_shape=jax.ShapeDtypeStruct((B * DIFF_HEADS, nk, VT_ROWS, tk), BF16),
        compiler_params=_cparams(("arbitrary", "arbitrary", "arbitrary")),
        name="vprep",
    )(h)


def _attn_kernel(q_ref, k_ref, vt_ref, lam_ref, g_ref, o_ref, acc_ref, *, tq, lambda_init):
    i = pl.program_id(2)
    q = q_ref[...]
    lane = lax.broadcasted_iota(jnp.int32, q.shape, 1)
    zero = jnp.zeros_like(q)
    qs = (jnp.where(lane < DIFF_QK_DIM, q, zero), jnp.where(lane >= DIFF_QK_DIM, q, zero))
    acc_ref[...] = jnp.zeros(acc_ref.shape, F32)
    neg = jnp.full((1, tq), -jnp.inf, F32)

    def block(j, ms, masked):
        kb = k_ref[pl.ds(pl.multiple_of(j * tq, tq), tq), :]
        vb = vt_ref[j]
        s = [lax.dot_general(kb, qs[t], (((1,), (1,)), ((), ())), preferred_element_type=F32)
             for t in range(2)]
        if masked:
            kr = lax.broadcasted_iota(jnp.int32, (tq, tq), 0)
            qc = lax.broadcasted_iota(jnp.int32, (tq, tq), 1)
            allowed = kr // CHUNK <= qc // CHUNK
        new = []
        for t in range(2):
            st = jnp.where(allowed, s[t], -jnp.inf) if masked else s[t]
            m_new = jnp.maximum(ms[t], jnp.max(st, axis=0, keepdims=True))
            alpha = jnp.exp2(ms[t] - m_new)
            p = jnp.exp2(st - m_new).astype(BF16)
            acc_ref[t] = alpha * acc_ref[t] + jnp.dot(vb, p, preferred_element_type=F32)
            new.append(m_new)
        return tuple(new)

    ms = lax.fori_loop(0, i, lambda j, ms: block(j, ms, False), (neg, neg))
    block(i, ms, True)

    lam_v = lam_ref[...]
    lam = (jnp.exp(jnp.sum(lam_v[0:1] * lam_v[1:2])) - jnp.exp(jnp.sum(lam_v[2:3] * lam_v[3:4]))
           + lambda_init)
    a0 = acc_ref[0]
    a1 = acc_ref[1]
    o = a0[:HEAD_DIM] / a0[HEAD_DIM:HEAD_DIM + 1] - lam * (a1[:HEAD_DIM] / a1[HEAD_DIM:HEAD_DIM + 1])
    y = o * lax.rsqrt(jnp.mean(o * o, axis=0, keepdims=True) + RMS_EPS) * g_ref[...]
    o_ref[...] = (y * (1.0 - lambda_init)).T.astype(o_ref.dtype)


def diff_attention(qk, vt, lam_params, norm_g, B, S, lambda_init):
    T = qk.shape[0]
    tq = min(ATT_TILE, S)
    nq = S // tq
    return pl.pallas_call(
        functools.partial(_attn_kernel, tq=tq, lambda_init=lambda_init),
        grid=(B, DIFF_HEADS, nq),
        in_specs=[pl.BlockSpec((tq, LANES), lambda b, hd, i: (b * nq + i, hd)),
                  pl.BlockSpec((S, LANES), lambda b, hd, i: (b, DIFF_HEADS + hd)),
                  pl.BlockSpec((None, nq, VT_ROWS, tq), lambda b, hd, i: (b * DIFF_HEADS + hd, 0, 0, 0)),
                  pl.BlockSpec((4, DIFF_QK_DIM), lambda b, hd, i: (0, 0)),
                  pl.BlockSpec((HEAD_DIM, 1), lambda b, hd, i: (0, 0))],
        out_specs=pl.BlockSpec((tq, LANES), lambda b, hd, i: (b * nq + i, hd)),
        out_shape=jax.ShapeDtypeStruct((T, HW), BF16),
        scratch_shapes=[pltpu.VMEM((2, VT_ROWS, tq), F32)],
        compiler_params=_cparams(("arbitrary", "arbitrary", "arbitrary")),
        name="diff_attn",
    )(qk, qk, vt, lam_params, norm_g.reshape(HEAD_DIM, 1))


def _silu(t):
    return t * jax.nn.sigmoid(t)


def _softplus(t):
    return jnp.maximum(t, 0.0) + jnp.log1p(jnp.exp(-jnp.abs(t)))


def _dot32(a, b):
    return jnp.dot(a, b, preferred_element_type=F32, precision=HIGHEST)


def _mm(a, b):
    return jnp.dot(a.astype(BF16), b.astype(BF16), preferred_element_type=F32)


def _mm_nt(a, b):
    return lax.dot_general(a.astype(BF16), b.astype(BF16), (((1,), (1,)), ((), ())),
                           preferred_element_type=F32)


DN_ROWS = 512
DN_HB = 2


def _gates_kernel(ab_ref, alog_ref, dt_ref, o_ref, *, rows):
    ab = ab_ref[...].astype(F32)
    g_all = -jnp.exp(alog_ref[...]) * _softplus(ab + dt_ref[...])
    beta_all = jax.nn.sigmoid(ab)
    ri = lax.broadcasted_iota(jnp.int32, (CHUNK, CHUNK), 0)
    ci = lax.broadcasted_iota(jnp.int32, (CHUNK, CHUNK), 1)
    tri = (ri >= ci).astype(F32)
    is_decay_lane = lax.broadcasted_iota(jnp.int32, (CHUNK, LANES), 1) < DN_HEADS
    for c in range(rows // CHUNK):
        sl = slice(c * CHUNK, (c + 1) * CHUNK)
        o_ref[sl, :] = jnp.where(is_decay_lane, _dot32(tri, g_all[sl]), beta_all[sl])


def dn_gates(h, a_log, dt_bias):
    T = h.shape[0]
    rows = min(512, T)
    alog_row = jnp.zeros((1, LANES), F32).at[0, :DN_HEADS].set(a_log)
    dt_row = jnp.zeros((1, LANES), F32).at[0, :DN_HEADS].set(dt_bias)
    row = pl.BlockSpec((1, LANES), lambda i: (0, 0))
    return pl.pallas_call(
        functools.partial(_gates_kernel, rows=rows),
        grid=(T // rows,),
        in_specs=[pl.BlockSpec((rows, LANES), lambda i: (i, COL_AB)), row, row],
        out_specs=pl.BlockSpec((rows, LANES), lambda i: (i, 0)),
        out_shape=jax.ShapeDtypeStruct((T, LANES), F32),
        compiler_params=_cparams(("arbitrary",)),
        name="dn_gates",
    )(h, alog_row, dt_row)


def _dn_kernel(q_ref, k_ref, v_ref, gate_ref, gb_ref, cwq_ref, cwk_ref, cwv_ref, ng_ref, o_ref,
               ext_ref, tail_ref, state_ref, *, rows, hb):
    hd0 = pl.program_id(1) * hb
    r = pl.program_id(2)
    pad = 8
    nc = rows // CHUNK

    @pl.when(r == 0)
    def _():
        tail_ref[...] = jnp.zeros(tail_ref.shape, F32)
        state_ref[...] = jnp.zeros(state_ref.shape, F32)

    def conv_silu(x_ref, cw_ref, slot):
        x = x_ref[...].astype(F32)
        ext_ref[0:pad, :] = tail_ref[slot]
        ext_ref[pad:pad + rows, :] = x
        tail_ref[slot] = x[rows - pad:rows]
        cw = cw_ref[...]
        acc = None
        for t in range(DN_CONV):
            off = pad - (DN_CONV - 1) + t
            term = ext_ref[off:off + rows, :] * cw[t:t + 1, :]
            acc = term if acc is None else acc + term
        return _silu(acc)

    def l2n(t):
        return t * lax.rsqrt(jnp.sum(t * t, axis=-1, keepdims=True) + RMS_EPS)

    qf = conv_silu(q_ref, cwq_ref, 0)
    kf = conv_silu(k_ref, cwk_ref, 1)
    vf = conv_silu(v_ref, cwv_ref, 2)
    gb = gb_ref[...]
    lane = lax.broadcasted_iota(jnp.int32, gb.shape, 1)

    ri = lax.broadcasted_iota(jnp.int32, (CHUNK, CHUNK), 0)
    ci = lax.broadcasted_iota(jnp.int32, (CHUNK, CHUNK), 1)
    causal = ri >= ci
    strict = ri > ci
    eye = (ri == ci).astype(F32)

    units = []
    for hh in range(hb):
        hl = slice(hh * HEAD_DIM, (hh + 1) * HEAD_DIM)
        q = l2n(qf[:, hl]) * (HEAD_DIM ** -0.5)
        k = l2n(kf[:, hl])
        gc_col = jnp.sum(jnp.where(lane == hd0 + hh, gb, 0.0), axis=-1, keepdims=True)
        beta = jnp.sum(jnp.where(lane == hd0 + hh + DN_HEADS, gb, 0.0), axis=-1, keepdims=True)
        gcb = jnp.broadcast_to(gc_col, (rows, HEAD_DIM))
        eg = jnp.exp(gcb)
        kb = k * beta
        rhs = jnp.concatenate([vf[:, hl] * beta, kb * eg], axis=-1)
        qg = q * eg
        for c in range(nc):
            sl = slice(c * CHUNK, (c + 1) * CHUNK)
            units.append(dict(q=q[sl], k=k[sl], kb=kb[sl], rhs=rhs[sl], qg=qg[sl], g=gcb[sl]))
    for un in units:
        g = un["g"]
        gdiff = g[:, :CHUNK] - g.T[:CHUNK, :]
        un["decay"] = jnp.exp(jnp.where(causal, gdiff, -jnp.inf))
    for un in units:
        un["m"] = jnp.where(strict, _mm_nt(un["kb"], un["k"]) * un["decay"], 0.0)
    for un in units:
        un["tinv"] = eye - un["m"]
        un["pw"] = _mm(un["m"], un["m"])
    for it in range(5):
        for un in units:
            un["tinv"] = un["tinv"] + _mm(un["tinv"], un["pw"])
        if it < 4:
            for un in units:
                un["pw"] = _mm(un["pw"], un["pw"])
    for un in units:
        un["sol"] = _mm(un["tinv"], un["rhs"])
        un["attn"] = _mm_nt(un["q"], un["k"]) * un["decay"]

    gate = gate_ref[...].astype(F32)
    ng = ng_ref[...]
    states = [state_ref[hh] for hh in range(hb)]
    for c in range(nc):
        sl = slice(c * CHUNK, (c + 1) * CHUNK)
        for hh in range(hb):
            un = units[hh * nc + c]
            hl = slice(hh * HEAD_DIM, (hh + 1) * HEAD_DIM)
            u, w = un["sol"][:, :HEAD_DIM], un["sol"][:, HEAD_DIM:]
            g = un["g"]
            g_last = g[CHUNK - 1:CHUNK, :]
            sb = states[hh].astype(BF16)
            v_new = u - jnp.dot(w.astype(BF16), sb, preferred_element_type=F32)
            vnb = v_new.astype(BF16)
            o = (jnp.dot(un["qg"].astype(BF16), sb, preferred_element_type=F32)
                 + jnp.dot(un["attn"].astype(BF16), vnb, preferred_element_type=F32))
            k_dec = un["k"] * jnp.exp(g_last - g)
            states[hh] = states[hh] * jnp.exp(g_last) + jnp.dot(k_dec.T.astype(BF16), vnb,
                                                                preferred_element_type=F32)
            y = o * lax.rsqrt(jnp.mean(o * o, axis=-1, keepdims=True) + RMS_EPS) * ng
            o_ref[sl, hl] = (y * _silu(gate[sl, hl])).astype(o_ref.dtype)
    for hh in range(hb):
        state_ref[hh] = states[hh]


def deltanet(h, gb, conv_w, norm_g, B, S):
    T = h.shape[0]
    rows = min(DN_ROWS, S)
    nr = S // rows
    hb = DN_HB
    width = hb * HEAD_DIM

    def col(c):
        return pl.BlockSpec((rows, width), lambda b, hg, r: (b * nr + r, c // hb + hg))

    def cw(c):
        return pl.BlockSpec((DN_CONV, width), lambda b, hg, r: (0, c // hb + hg))

    return pl.pallas_call(
        functools.partial(_dn_kernel, rows=rows, hb=hb),
        grid=(B, DN_HEADS // hb, nr),
        in_specs=[col(COL_BQ), col(COL_BK), col(COL_BV), col(COL_BG),
                  pl.BlockSpec((rows, LANES), lambda b, hg, r: (b * nr + r, 0)),
                  cw(0), cw(DN_HEADS), cw(2 * DN_HEADS),
                  pl.BlockSpec((1, LANES), lambda b, hg, r: (0, 0))],
        out_specs=pl.BlockSpec((rows, width), lambda b, hg, r: (b * nr + r, hg)),
        out_shape=jax.ShapeDtypeStruct((T, HW), BF16),
        scratch_shapes=[pltpu.VMEM((rows + 8, width), F32), pltpu.VMEM((3, 8, width), F32),
                        pltpu.VMEM((hb, HEAD_DIM, HEAD_DIM), F32)],
        compiler_params=_cparams(("arbitrary", "arbitrary", "arbitrary")),
        name="deltanet",
    )(h, h, h, h, gb, conv_w, conv_w, conv_w, norm_g.reshape(1, LANES))


def _layer_norm(z, g, b):
    mu = jnp.mean(z, axis=-1, keepdims=True)
    var = jnp.mean(jnp.square(z - mu), axis=-1, keepdims=True)
    return (z - mu) * lax.rsqrt(var + LN_EPS) * g + b


def _outproj_kernel(ya_ref, yb_ref, yc_ref, x_ref, wa_ref, wb_ref, wc_ref, g_ref, b_ref, rwh_ref, rwl_ref,
                    rb_ref, x1_ref, xp_ref, rt_ref):
    mixed = (jnp.dot(ya_ref[...], wa_ref[...], preferred_element_type=F32)
             + jnp.dot(yb_ref[...], wb_ref[...], preferred_element_type=F32)
             + jnp.dot(yc_ref[...], wc_ref[...], preferred_element_type=F32))
    x1 = _layer_norm(DEEPNORM_ALPHA * x_ref[...] + mixed, g_ref[...], b_ref[...])
    x1_ref[...] = x1
    _store_planes(xp_ref, _pack_rows(x1))

    x_hi = x1.astype(BF16)
    x_lo = (x1 - x_hi.astype(F32)).astype(BF16)
    logits = (jnp.dot(x_hi, rwh_ref[...], preferred_element_type=F32)
              + jnp.dot(x_lo, rwh_ref[...], preferred_element_type=F32)
              + jnp.dot(x_hi, rwl_ref[...], preferred_element_type=F32))
    scores = jax.nn.sigmoid(logits)
    tm = scores.shape[0]
    lane = lax.broadcasted_iota(jnp.int32, scores.shape, 1)
    lanef = lane.astype(F32)
    neg = -jnp.inf
    big = float(LANES)
    sel = jnp.where(lane < N_EXPERTS, scores + rb_ref[...], neg)

    def top2(mg):
        t1 = jnp.max(mg, axis=-1, keepdims=True)
        i1 = jnp.min(jnp.where(mg == t1, lanef, big), axis=-1, keepdims=True)
        mg2 = jnp.where(lanef == i1, neg, mg)
        t2 = jnp.max(mg2, axis=-1, keepdims=True)
        i2 = jnp.min(jnp.where(mg2 == t2, lanef, big), axis=-1, keepdims=True)
        return t1 + t2, i1, i2

    best_s = best_1 = best_2 = None
    for gidx in range(N_EXPERT_GROUPS):
        in_g = (lane >= gidx * EXPERTS_PER_GROUP) & (lane < (gidx + 1) * EXPERTS_PER_GROUP)
        gs, i1, i2 = top2(jnp.where(in_g, sel, neg))
        if best_s is None:
            best_s, best_1, best_2 = gs, i1, i2
        else:
            take = gs > best_s
            best_s = jnp.where(take, gs, best_s)
            best_1 = jnp.where(take, i1, best_1)
            best_2 = jnp.where(take, i2, best_2)
    s1 = jnp.sum(jnp.where(lanef == best_1, scores, 0.0), axis=-1, keepdims=True)
    s2 = jnp.sum(jnp.where(lanef == best_2, scores, 0.0), axis=-1, keepdims=True)
    tot = s1 + s2
    l8 = lax.broadcasted_iota(jnp.int32, (tm, 8), 1)
    rt_ref[...] = jnp.where(l8 == 0, best_1, jnp.where(l8 == 1, best_2, jnp.where(
        l8 == 2, s1 / tot, jnp.where(l8 == 3, s2 / tot, 0.0))))


def out_proj_ln_router(ya, yb, yc, x, w_out, ln_g, ln_b, rw_pad, rb_pad):
    T = x.shape[0]
    tm = min(256, T)
    wa = w_out[:SGU_WIDTH].astype(BF16)
    wb = w_out[SGU_WIDTH:SGU_WIDTH + HW].astype(BF16)
    wc = w_out[SGU_WIDTH + HW:].astype(BF16)
    rw_hi = rw_pad.astype(BF16)
    rw_lo = (rw_pad - rw_hi.astype(F32)).astype(BF16)

    def rowblk(width):
        return pl.BlockSpec((tm, width), lambda i: (i, 0))

    def full(shape):
        return pl.BlockSpec(shape, lambda i: (0, 0))

    return pl.pallas_call(
        _outproj_kernel,
        grid=(T // tm,),
        in_specs=[rowblk(SGU_WIDTH), rowblk(HW), rowblk(HW), rowblk(D_MODEL),
                  full((SGU_WIDTH, D_MODEL)), full((HW, D_MODEL)), full((HW, D_MODEL)),
                  full((1, D_MODEL)), full((1, D_MODEL)), full((D_MODEL, LANES)), full((D_MODEL, LANES)),
                  full((1, LANES))],
        out_specs=[rowblk(D_MODEL), pl.BlockSpec((N_PLANES, tm, PLANE_W), lambda i: (0, i, 0)), rowblk(8)],
        out_shape=[jax.ShapeDtypeStruct((T, D_MODEL), F32),
                   jax.ShapeDtypeStruct((N_PLANES, T, PLANE_W), jnp.uint32), jax.ShapeDtypeStruct((T, 8), F32)],
        compiler_params=_cparams(("arbitrary",)),
        name="out_proj_ln_router",
    )(ya, yb, yc, x, wa, wb, wc, ln_g.reshape(1, D_MODEL), ln_b.reshape(1, D_MODEL), rw_hi, rw_lo, rb_pad)


def _route_tables(eidx, T, nb):
    A = T * TOP_K
    flat_e = eidx.reshape(A)
    order = jnp.argsort(flat_e, stable=True).astype(jnp.int32)
    experts = jnp.arange(N_EXPERTS, dtype=jnp.int32)
    counts = jnp.sum((flat_e[:, None] == experts[None, :]).astype(jnp.int32), axis=0)
    padded = (counts + MOE_BLK - 1) // MOE_BLK * MOE_BLK
    pad_end = jnp.cumsum(padded)
    pad_start = pad_end - padded
    start = jnp.cumsum(counts) - counts
    pos = jnp.arange(nb * MOE_BLK, dtype=jnp.int32)
    e_of_pos = jnp.sum((pos[:, None] >= pad_end[None, :]).astype(jnp.int32), axis=1)
    onehot = e_of_pos[:, None] == experts[None, :]

    def lookup(tab):
        return jnp.sum(jnp.where(onehot, tab[None, :], 0), axis=1)

    rank = pos - lookup(pad_start)
    valid = rank < lookup(counts)
    a_sorted = order[jnp.clip(lookup(start) + rank, 0, A - 1)]
    src = jnp.where(valid, a_sorted // TOP_K, 0)
    block_expert = jnp.minimum(e_of_pos.reshape(nb, MOE_BLK)[:, 0], N_EXPERTS - 1)
    n_valid = jnp.sum(valid.reshape(nb, MOE_BLK).astype(jnp.int32), axis=1)
    onehot_a = flat_e[:, None] == experts[None, :]
    rank_a = jnp.argsort(order).astype(jnp.int32) - jnp.sum(jnp.where(onehot_a, start[None, :], 0), axis=1)
    pos_a = jnp.sum(jnp.where(onehot_a, pad_start[None, :], 0), axis=1) + rank_a
    inv = pos_a.reshape(T, TOP_K).T.reshape(A)
    return block_expert, n_valid, src, inv


def _moe_kernel(be_ref, nv_ref, x_ref, wg_ref, wu_ref, wd_ref, o_ref):
    n_valid = nv_ref[pl.program_id(0)]

    @pl.when(n_valid > 0)
    def _():
        xb = _unpack_rows(_load_planes(x_ref)).astype(BF16)
        hg = jnp.dot(xb, wg_ref[...], preferred_element_type=F32)
        hu = jnp.dot(xb, wu_ref[...], preferred_element_type=F32)
        hb = (_silu(hg) * hu).astype(BF16)
        _store_planes(o_ref, _pack_rows(jnp.dot(hb, wd_ref[...], preferred_element_type=F32)))

    @pl.when(n_valid == 0)
    def _():
        o_ref[...] = jnp.zeros(o_ref.shape, o_ref.dtype)


def moe(xp, eidx, wg, wu, wd, layer):
    T = xp.shape[1]
    A = T * TOP_K
    nb = -(-A // MOE_BLK) + N_EXPERTS
    block_expert, n_valid, src, inv = _route_tables(eidx, T, nb)
    xs = _gather_planes(xp, src)

    def w_blk(rows, cols):
        return pl.BlockSpec((None, None, rows, cols), lambda b, be, nv: (layer, be[b], 0, 0))

    row_blk = pl.BlockSpec((N_PLANES, MOE_BLK, PLANE_W), lambda b, be, nv: (0, b, 0))
    grid_spec = pltpu.PrefetchScalarGridSpec(
        num_scalar_prefetch=2,
        grid=(nb,),
        in_specs=[row_blk, w_blk(D_MODEL, D_EXPERT), w_blk(D_MODEL, D_EXPERT), w_blk(D_EXPERT, D_MODEL)],
        out_specs=row_blk,
    )
    ys = pl.pallas_call(
        _moe_kernel,
        grid_spec=grid_spec,
        out_shape=jax.ShapeDtypeStruct((N_PLANES, nb * MOE_BLK, PLANE_W), jnp.uint32),
        compiler_params=_cparams(("arbitrary",)),
        name="moe",
    )(block_expert, n_valid, xs, wg, wu, wd)
    return _gather_planes(ys, inv)


def _combine_kernel(x_ref, y0_ref, y1_ref, rt_ref, g_ref, b_ref, o_ref, ob_ref):
    rt = rt_ref[...]
    ffn = _unpack_rows(_load_planes(y0_ref)) * rt[:, 2:3] + _unpack_rows(_load_planes(y1_ref)) * rt[:, 3:4]
    x2 = _layer_norm(DEEPNORM_ALPHA * x_ref[...] + ffn, g_ref[...], b_ref[...])
    o_ref[...] = x2
    ob_ref[...] = x2.astype(BF16)


def combine_ln(x1, y, rt, ln_g, ln_b):
    T = x1.shape[0]
    tm = min(256, T)
    nt = T // tm
    blk = pl.BlockSpec((tm, D_MODEL), lambda i: (i, 0))
    vec = pl.BlockSpec((1, D_MODEL), lambda i: (0, 0))
    return pl.pallas_call(
        _combine_kernel,
        grid=(nt,),
        in_specs=[blk, pl.BlockSpec((N_PLANES, tm, PLANE_W), lambda i: (0, i, 0)),
                  pl.BlockSpec((N_PLANES, tm, PLANE_W), lambda i: (0, nt + i, 0)),
                  pl.BlockSpec((tm, 8), lambda i: (i, 0)), vec, vec],
        out_specs=[blk, blk],
        out_shape=[jax.ShapeDtypeStruct((T, D_MODEL), F32), jax.ShapeDtypeStruct((T, D_MODEL), BF16)],
        compiler_params=_cparams(("arbitrary",)),
        name="combine_ln",
    )(x1, y, y, rt, ln_g.reshape(1, D_MODEL), ln_b.reshape(1, D_MODEL))


def _repack_kernel(sa_ref, sb_ref, cls_ref, a_ref, b_ref, o_ref):
    cls = cls_ref[pl.program_id(0)]
    lane = lax.broadcasted_iota(jnp.int32, a_ref.shape, 1)
    shift = AB_SRC_COL % LANES + 2 * DN_HEADS

    @pl.when(cls == 0)
    def _():
        o_ref[...] = a_ref[...].astype(o_ref.dtype)

    @pl.when(cls == 1)
    def _():
        left = pltpu.roll(a_ref[...], LANES - shift, axis=1)
        right = pltpu.roll(b_ref[...], LANES - shift, axis=1)
        o_ref[...] = jnp.where(lane < LANES - shift, left, right).astype(o_ref.dtype)

    @pl.when(cls == 2)
    def _():
        o_ref[...] = jnp.where(lane < 2 * DN_HEADS, a_ref[...], 0.0).astype(o_ref.dtype)


def repack_w_in(w_in, layer):
    n_aligned = AB_SRC_COL // LANES
    sa = np.array(list(range(n_aligned)) + [n_aligned + j for j in range(COL_AB - n_aligned)] + [n_aligned],
                  np.int32)
    sb = np.minimum(sa + 1, w_in.shape[2] // LANES).astype(np.int32)
    cls = np.array([0] * n_aligned + [1] * (COL_AB - n_aligned) + [2], np.int32)
    rows = w_in.shape[1]
    grid_spec = pltpu.PrefetchScalarGridSpec(
        num_scalar_prefetch=3,
        grid=(NW // LANES,),
        in_specs=[pl.BlockSpec((None, rows, LANES), lambda j, sa, sb, cls: (layer, 0, sa[j])),
                  pl.BlockSpec((None, rows, LANES), lambda j, sa, sb, cls: (layer, 0, sb[j]))],
        out_specs=pl.BlockSpec((rows, LANES), lambda j, sa, sb, cls: (0, j)),
    )
    return pl.pallas_call(
        _repack_kernel,
        grid_spec=grid_spec,
        out_shape=jax.ShapeDtypeStruct((rows, NW), BF16),
        compiler_params=_cparams(("arbitrary",)),
        name="repack_w_in",
    )(jnp.asarray(sa), jnp.asarray(sb), jnp.asarray(cls), w_in, w_in)


def kernel(x, w_in, sgu_norm_g, sgu_norm_b, sgu_w, sgu_b, dn_conv_w, dn_a_log, dn_dt_bias, dn_norm_g,
           diff_lambda_q1, diff_lambda_k1, diff_lambda_q2, diff_lambda_k2, diff_norm_g, w_out, ln1_g, ln1_b,
           router_w, router_bias, moe_w_gate, moe_w_up, moe_w_down, ln2_g, ln2_b):
    B, S, D = x.shape
    T = B * S
    xf = x.reshape(T, D)
    xb = xf.astype(BF16)
    cos_t, sin_t = _rope_tables(S)
    rw_pad = jnp.zeros((D, LANES), F32).at[:, :N_EXPERTS].set(router_w)
    rb_pad = jnp.zeros((1, LANES), F32).at[0, :N_EXPERTS].set(router_bias)
    wg, wu, wd = moe_w_gate.astype(BF16), moe_w_up.astype(BF16), moe_w_down.astype(BF16)
    for l in range(DEPTH):
        h = in_proj(xb, repack_w_in(w_in, l))
        y_a = sgu(h, sgu_norm_g[l], sgu_norm_b[l], sgu_w[l], sgu_b[l])
        y_b = deltanet(h, dn_gates(h, dn_a_log[l], dn_dt_bias[l]), dn_conv_w[l], dn_norm_g[l], B, S)
        qk = rope(h, cos_t, sin_t)
        lam_params = jnp.stack([diff_lambda_q1[l], diff_lambda_k1[l], diff_lambda_q2[l], diff_lambda_k2[l]])
        lambda_init = 0.8 - 0.6 * math.exp(-0.3 * l)
        y_c = diff_attention(qk, vprep(h, B, S), lam_params, diff_norm_g[l], B, S, lambda_init)
        x1, xp, rt = out_proj_ln_router(y_a, y_b, y_c, xf, w_out[l], ln1_g[l], ln1_b[l], rw_pad, rb_pad)
        eidx = rt[:, :TOP_K].astype(jnp.int32)
        y = moe(xp, eidx, wg, wu, wd, l)
        xf, xb = combine_ln(x1, y, rt, ln2_g[l], ln2_b[l])
    return xf.reshape(B, S, D)
```

```python
import functools
import math

import numpy as np
import jax
import jax.numpy as jnp
from jax import lax
from jax.experimental import pallas as pl
from jax.experimental.pallas import tpu as pltpu
from jax.experimental.pallas import tpu_sc as plsc

F32 = jnp.float32
BF16 = jnp.bfloat16
HIGHEST = lax.Precision.HIGHEST

D_MODEL = 2048
DEPTH = 2
CHUNK = 64
HEAD_DIM = 128
SGU_GROUPS = 4
SGU_WIDTH = SGU_GROUPS * HEAD_DIM
SGU_BLOCK = 128
DN_HEADS = 6
DN_CONV = 4
DIFF_HEADS = 6
DIFF_QK_DIM = 64
ROPE_THETA = 10000.0
N_EXPERTS = 32
N_EXPERT_GROUPS = 4
EXPERTS_PER_GROUP = N_EXPERTS // N_EXPERT_GROUPS
TOP_K = 2
D_EXPERT = D_MODEL // 2
DEEPNORM_ALPHA = (2 * DEPTH) ** 0.25
LN_EPS = 1e-5
RMS_EPS = 1e-6

LANES = 128
VMEM_LIMIT = 56 * 1024 * 1024

HW = DN_HEADS * HEAD_DIM
COL_AU, COL_AV = 0, 4
COL_BQ, COL_BK, COL_BV, COL_BG = 8, 14, 20, 26
COL_CQ, COL_CK, COL_CV = 32, 38, 44
COL_AB = 50
AB_SRC_COL = 2 * SGU_WIDTH + 4 * HW
NW = 51 * LANES
N_TILES_IN = 3

MOE_BLK = 256


def _cparams(sem, vmem=VMEM_LIMIT):
    return pltpu.CompilerParams(dimension_semantics=sem, vmem_limit_bytes=vmem)


HALF_D = D_MODEL // 2


def _pack_rows(x):
    lo = lax.bitcast_convert_type(x[:, :HALF_D].astype(BF16).astype(F32), jnp.uint32)
    hi = lax.bitcast_convert_type(x[:, HALF_D:].astype(BF16).astype(F32), jnp.uint32)
    return (hi & jnp.uint32(0xFFFF0000)) | (lo >> 16)


def _unpack_rows(w):
    lo = lax.bitcast_convert_type(w << 16, F32)
    hi = lax.bitcast_convert_type(w & jnp.uint32(0xFFFF0000), F32)
    return jnp.concatenate([lo, hi], axis=1)


N_PLANES = 4
PLANE_W = HALF_D // N_PLANES


def _store_planes(ref, packed):
    for j in range(N_PLANES):
        ref[j] = packed[:, j * PLANE_W:(j + 1) * PLANE_W]


def _load_planes(ref):
    return jnp.concatenate([ref[j] for j in range(N_PLANES)], axis=1)


SC_WINDOW = 128


def sc_gather(x, idx):
    M = idx.shape[0]
    W = x.shape[1]
    mesh = plsc.VectorSubcoreMesh(core_axis_name="core", subcore_axis_name="subcore")

    @functools.partial(pl.kernel, out_type=jax.ShapeDtypeStruct((M, W), x.dtype), mesh=mesh, scratch_types=[])
    def gather_kernel(x_hbm, i_hbm, o_hbm):
        def body(i_vmem, o_vmem):
            pltpu.sync_copy(x_hbm.at[i_vmem.at[0]], o_vmem)

        pltpu.emit_pipeline(
            body,
            grid=(M // SC_WINDOW,),
            in_specs=[pl.BlockSpec((1, SC_WINDOW), index_map=lambda i: (0, i))],
            out_specs=[pl.BlockSpec((SC_WINDOW, W), index_map=lambda i: (i, 0))],
            core_axis_name=("core", "subcore"),
            dimension_semantics=(pltpu.PARALLEL,),
        )(i_hbm, o_hbm)

    return gather_kernel(x, idx.reshape(1, M))


def _gather_planes(x, idx):
    n = x.shape[1]
    flat_idx = (jnp.arange(N_PLANES, dtype=jnp.int32)[:, None] * n + idx[None, :]).reshape(-1)
    return sc_gather(x.reshape(N_PLANES * n, PLANE_W), flat_idx).reshape(N_PLANES, idx.shape[0], PLANE_W)


def _matmul_kernel(x_ref, w_ref, o_ref):
    o_ref[...] = jnp.dot(x_ref[...], w_ref[...], preferred_element_type=F32).astype(o_ref.dtype)


def in_proj(xb, w):
    T = xb.shape[0]
    tm = min(512, T)
    tn = NW // N_TILES_IN
    return pl.pallas_call(
        _matmul_kernel,
        grid=(N_TILES_IN, T // tm),
        in_specs=[pl.BlockSpec((tm, D_MODEL), lambda j, i: (i, 0)),
                  pl.BlockSpec((D_MODEL, tn), lambda j, i: (0, j))],
        out_specs=pl.BlockSpec((tm, tn), lambda j, i: (i, j)),
        out_shape=jax.ShapeDtypeStruct((T, NW), BF16),
        compiler_params=_cparams(("arbitrary", "arbitrary")),
        name="in_proj",
    )(xb, w)


def _gelu(t):
    return t * (lax.erf(t * (2.0 ** -0.5)) + 1.0) * 0.5


def _sgu_kernel(u_ref, v_ref, ng_ref, nb_ref, w_ref, b_ref, o_ref, *, rows):
    u = _gelu(u_ref[...].astype(F32))
    v = _gelu(v_ref[...].astype(F32))
    mu = jnp.mean(v, axis=-1, keepdims=True)
    var = jnp.mean(jnp.square(v - mu), axis=-1, keepdims=True)
    vn = ((v - mu) * lax.rsqrt(var + LN_EPS) * ng_ref[...] + nb_ref[...]).astype(BF16)
    ii = lax.broadcasted_iota(jnp.int32, (SGU_BLOCK, SGU_BLOCK), 0)
    jj = lax.broadcasted_iota(jnp.int32, (SGU_BLOCK, SGU_BLOCK), 1)
    w = jnp.where(jj // CHUNK <= ii // CHUNK, w_ref[...], 0.0).astype(BF16)
    bias = b_ref[...]
    for n in range(rows // SGU_BLOCK):
        sl = slice(n * SGU_BLOCK, (n + 1) * SGU_BLOCK)
        s = jnp.dot(w, vn[sl], preferred_element_type=F32) + bias
        o_ref[sl, :] = (u[sl] * s).astype(o_ref.dtype)


def sgu(h, norm_g, norm_b, w_s, b_s):
    T = h.shape[0]
    rows = min(1024, T)
    return pl.pallas_call(
        functools.partial(_sgu_kernel, rows=rows),
        grid=(T // rows, SGU_GROUPS),
        in_specs=[pl.BlockSpec((rows, LANES), lambda i, g: (i, COL_AU + g)),
                  pl.BlockSpec((rows, LANES), lambda i, g: (i, COL_AV + g)),
                  pl.BlockSpec((1, LANES), lambda i, g: (0, g)),
                  pl.BlockSpec((1, LANES), lambda i, g: (0, g)),
                  pl.BlockSpec((None, SGU_BLOCK, SGU_BLOCK), lambda i, g: (g, 0, 0)),
                  pl.BlockSpec((None, SGU_BLOCK, 1), lambda i, g: (g, 0, 0))],
        out_specs=pl.BlockSpec((rows, LANES), lambda i, g: (i, g)),
        out_shape=jax.ShapeDtypeStruct((T, SGU_WIDTH), BF16),
        compiler_params=_cparams(("arbitrary", "arbitrary")),
        name="sgu",
    )(h, h, norm_g.reshape(1, SGU_WIDTH), norm_b.reshape(1, SGU_WIDTH), w_s,
      b_s.reshape(SGU_GROUPS, SGU_BLOCK, 1))


def _rope_kernel(x_ref, cos_ref, sin_ref, o_ref):
    half = DIFF_QK_DIM // 2
    scale = jnp.where(pl.program_id(1) < DIFF_HEADS // ROPE_HEADS, DIFF_QK_DIM ** -0.5 * math.log2(math.e), 1.0)
    cos = cos_ref[...]
    sin = sin_ref[...]
    lane = lax.broadcasted_iota(jnp.int32, cos.shape, 1)
    for hh in range(ROPE_HEADS):
        hl = slice(hh * LANES, (hh + 1) * LANES)
        x = x_ref[:, hl].astype(F32)
        partner = jnp.where(lane % DIFF_QK_DIM < half,
                            pltpu.roll(x, LANES - half, axis=1), pltpu.roll(x, half, axis=1))
        o_ref[:, hl] = ((x * cos + partner * sin) * scale).astype(o_ref.dtype)


def _rope_tables(S):
    half = DIFF_QK_DIM // 2
    inv_freq = ROPE_THETA ** (-jnp.arange(half, dtype=F32) / half)
    ang = jnp.arange(S, dtype=F32)[:, None] * inv_freq[None, :]
    cos, sin = jnp.cos(ang), jnp.sin(ang)
    cos_t = jnp.tile(cos, (1, LANES // half))
    sin_t = jnp.tile(jnp.concatenate([-sin, sin], axis=1), (1, LANES // DIFF_QK_DIM))
    return cos_t, sin_t


ROPE_HEADS = 2
ROPE_ROWS = 1024


def rope(h, cos_t, sin_t):
    T = h.shape[0]
    S = cos_t.shape[0]
    rows = min(ROPE_ROWS, S)
    ns = S // rows
    width = ROPE_HEADS * LANES
    return pl.pallas_call(
        _rope_kernel,
        grid=(T // rows, 2 * DIFF_HEADS // ROPE_HEADS),
        in_specs=[pl.BlockSpec((rows, width), lambda i, j: (i, COL_CQ // ROPE_HEADS + j)),
                  pl.BlockSpec((rows, LANES), lambda i, j: (i % ns, 0)),
                  pl.BlockSpec((rows, LANES), lambda i, j: (i % ns, 0))],
        out_specs=pl.BlockSpec((rows, width), lambda i, j: (i, j)),
        out_shape=jax.ShapeDtypeStruct((T, 2 * HW), BF16),
        compiler_params=_cparams(("arbitrary", "arbitrary")),
        name="rope",
    )(h, cos_t, sin_t)


ATT_TILE = 1024
VT_ROWS = HEAD_DIM + 16


def _vprep_kernel(v_ref, o_ref):
    vt = v_ref[...].astype(F32).T
    o_ref[0:HEAD_DIM, :] = vt.astype(o_ref.dtype)
    row = lax.broadcasted_iota(jnp.int32, (VT_ROWS - HEAD_DIM, vt.shape[1]), 0)
    o_ref[HEAD_DIM:VT_ROWS, :] = jnp.where(row == 0, 1.0, 0.0).astype(o_ref.dtype)


def vprep(h, B, S):
    tk = min(ATT_TILE, S)
    nk = S // tk
    return pl.pallas_call(
        _vprep_kernel,
        grid=(B, DIFF_HEADS, nk),
        in_specs=[pl.BlockSpec((tk, LANES), lambda b, hd, j: (b * nk + j, COL_CV + hd))],
        out_specs=pl.BlockSpec((None, None, VT_ROWS, tk), lambda b, hd, j: (b * DIFF_HEADS + hd, j, 0, 0)),
        out_shape=jax.ShapeDtypeStruct((B * DIFF_HEADS, nk, VT_ROWS, tk), BF16),
        compiler_params=_cparams(("arbitrary", "arbitrary", "arbitrary")),
        name="vprep",
    )(h)


def _attn_kernel(q_ref, k_ref, vt_ref, lam_ref, g_ref, o_ref, acc_ref, *, tq, lambda_init):
    i = pl.program_id(2)
    q = q_ref[...]
    lane = lax.broadcasted_iota(jnp.int32, q.shape, 1)
    zero = jnp.zeros_like(q)
    qs = (jnp.where(lane < DIFF_QK_DIM, q, zero), jnp.where(lane >= DIFF_QK_DIM, q, zero))
    acc_ref[...] = jnp.zeros(acc_ref.shape, F32)
    neg = jnp.full((1, tq), -jnp.inf, F32)

    def block(j, ms, masked):
        kb = k_ref[pl.ds(pl.multiple_of(j * tq, tq), tq), :]
        vb = vt_ref[j]
        s = [lax.dot_general(kb, qs[t], (((1,), (1,)), ((), ())), preferred_element_type=F32)
             for t in range(2)]
        if masked:
            kr = lax.broadcasted_iota(jnp.int32, (tq, tq), 0)
            qc = lax.broadcasted_iota(jnp.int32, (tq, tq), 1)
            allowed = kr // CHUNK <= qc // CHUNK
        new = []
        for t in range(2):
            st = jnp.where(allowed, s[t], -jnp.inf) if masked else s[t]
            m_new = jnp.maximum(ms[t], jnp.max(st, axis=0, keepdims=True))
            alpha = jnp.exp2(ms[t] - m_new)
            p = jnp.exp2(st - m_new).astype(BF16)
            acc_ref[t] = alpha * acc_ref[t] + jnp.dot(vb, p, preferred_element_type=F32)
            new.append(m_new)
        return tuple(new)

    ms = lax.fori_loop(0, i, lambda j, ms: block(j, ms, False), (neg, neg))
    block(i, ms, True)

    lam_v = lam_ref[...]
    lam = (jnp.exp(jnp.sum(lam_v[0:1] * lam_v[1:2])) - jnp.exp(jnp.sum(lam_v[2:3] * lam_v[3:4]))
           + lambda_init)
    a0 = acc_ref[0]
    a1 = acc_ref[1]
    o = a0[:HEAD_DIM] / a0[HEAD_DIM:HEAD_DIM + 1] - lam * (a1[:HEAD_DIM] / a1[HEAD_DIM:HEAD_DIM + 1])
    y = o * lax.rsqrt(jnp.mean(o * o, axis=0, keepdims=True) + RMS_EPS) * g_ref[...]
    o_ref[...] = (y * (1.0 - lambda_init)).T.astype(o_ref.dtype)


def diff_attention(qk, vt, lam_params, norm_g, B, S, lambda_init):
    T = qk.shape[0]
    tq = min(ATT_TILE, S)
    nq = S // tq
    return pl.pallas_call(
        functools.partial(_attn_kernel, tq=tq, lambda_init=lambda_init),
        grid=(B, DIFF_HEADS, nq),
        in_specs=[pl.BlockSpec((tq, LANES), lambda b, hd, i: (b * nq + i, hd)),
                  pl.BlockSpec((S, LANES), lambda b, hd, i: (b, DIFF_HEADS + hd)),
                  pl.BlockSpec((None, nq, VT_ROWS, tq), lambda b, hd, i: (b * DIFF_HEADS + hd, 0, 0, 0)),
                  pl.BlockSpec((4, DIFF_QK_DIM), lambda b, hd, i: (0, 0)),
                  pl.BlockSpec((HEAD_DIM, 1), lambda b, hd, i: (0, 0))],
        out_specs=pl.BlockSpec((tq, LANES), lambda b, hd, i: (b * nq + i, hd)),
        out_shape=jax.ShapeDtypeStruct((T, HW), BF16),
        scratch_shapes=[pltpu.VMEM((2, VT_ROWS, tq), F32)],
        compiler_params=_cparams(("arbitrary", "arbitrary", "arbitrary")),
        name="diff_attn",
    )(qk, qk, vt, lam_params, norm_g.reshape(HEAD_DIM, 1))


def _silu(t):
    return t * jax.nn.sigmoid(t)


def _softplus(t):
    return jnp.maximum(t, 0.0) + jnp.log1p(jnp.exp(-jnp.abs(t)))


def _dot32(a, b):
    return jnp.dot(a, b, preferred_element_type=F32, precision=HIGHEST)


def _mm(a, b):
    return jnp.dot(a.astype(BF16), b.astype(BF16), preferred_element_type=F32)


def _mm_nt(a, b):
    return lax.dot_general(a.astype(BF16), b.astype(BF16), (((1,), (1,)), ((), ())),
                           preferred_element_type=F32)


DN_ROWS = 512
DN_HB = 2


def _gates_kernel(ab_ref, alog_ref, dt_ref, o_ref, *, rows):
    ab = ab_ref[...].astype(F32)
    g_all = -jnp.exp(alog_ref[...]) * _softplus(ab + dt_ref[...])
    beta_all = jax.nn.sigmoid(ab)
    ri = lax.broadcasted_iota(jnp.int32, (CHUNK, CHUNK), 0)
    ci = lax.broadcasted_iota(jnp.int32, (CHUNK, CHUNK), 1)
    tri = (ri >= ci).astype(F32)
    is_decay_lane = lax.broadcasted_iota(jnp.int32, (CHUNK, LANES), 1) < DN_HEADS
    for c in range(rows // CHUNK):
        sl = slice(c * CHUNK, (c + 1) * CHUNK)
        o_ref[sl, :] = jnp.where(is_decay_lane, _dot32(tri, g_all[sl]), beta_all[sl])


def dn_gates(h, a_log, dt_bias):
    T = h.shape[0]
    rows = min(512, T)
    alog_row = jnp.zeros((1, LANES), F32).at[0, :DN_HEADS].set(a_log)
    dt_row = jnp.zeros((1, LANES), F32).at[0, :DN_HEADS].set(dt_bias)
    row = pl.BlockSpec((1, LANES), lambda i: (0, 0))
    return pl.pallas_call(
        functools.partial(_gates_kernel, rows=rows),
        grid=(T // rows,),
        in_specs=[pl.BlockSpec((rows, LANES), lambda i: (i, COL_AB)), row, row],
        out_specs=pl.BlockSpec((rows, LANES), lambda i: (i, 0)),
        out_shape=jax.ShapeDtypeStruct((T, LANES), F32),
        compiler_params=_cparams(("arbitrary",)),
        name="dn_gates",
    )(h, alog_row, dt_row)


def _dn_kernel(q_ref, k_ref, v_ref, gate_ref, gb_ref, cwq_ref, cwk_ref, cwv_ref, ng_ref, o_ref,
               ext_ref, tail_ref, state_ref, *, rows, hb):
    hd0 = pl.program_id(1) * hb
    r = pl.program_id(2)
    pad = 8
    nc = rows // CHUNK

    @pl.when(r == 0)
    def _():
        tail_ref[...] = jnp.zeros(tail_ref.shape, F32)
        state_ref[...] = jnp.zeros(state_ref.shape, F32)

    def conv_silu(x_ref, cw_ref, slot):
        x = x_ref[...].astype(F32)
        ext_ref[0:pad, :] = tail_ref[slot]
        ext_ref[pad:pad + rows, :] = x
        tail_ref[slot] = x[rows - pad:rows]
        cw = cw_ref[...]
        acc = None
        for t in range(DN_CONV):
            off = pad - (DN_CONV - 1) + t
            term = ext_ref[off:off + rows, :] * cw[t:t + 1, :]
            acc = term if acc is None else acc + term
        return _silu(acc)

    def l2n(t):
        return t * lax.rsqrt(jnp.sum(t * t, axis=-1, keepdims=True) + RMS_EPS)

    qf = conv_silu(q_ref, cwq_ref, 0)
    kf = conv_silu(k_ref, cwk_ref, 1)
    vf = conv_silu(v_ref, cwv_ref, 2)
    gb = gb_ref[...]
    lane = lax.broadcasted_iota(jnp.int32, gb.shape, 1)

    ri = lax.broadcasted_iota(jnp.int32, (CHUNK, CHUNK), 0)
    ci = lax.broadcasted_iota(jnp.int32, (CHUNK, CHUNK), 1)
    causal = ri >= ci
    strict = ri > ci
    eye = (ri == ci).astype(F32)

    units = []
    for hh in range(hb):
        hl = slice(hh * HEAD_DIM, (hh + 1) * HEAD_DIM)
        q = l2n(qf[:, hl]) * (HEAD_DIM ** -0.5)
        k = l2n(kf[:, hl])
        gc_col = jnp.sum(jnp.where(lane == hd0 + hh, gb, 0.0), axis=-1, keepdims=True)
        beta = jnp.sum(jnp.where(lane == hd0 + hh + DN_HEADS, gb, 0.0), axis=-1, keepdims=True)
        gcb = jnp.broadcast_to(gc_col, (rows, HEAD_DIM))
        eg = jnp.exp(gcb)
        kb = k * beta
        rhs = jnp.concatenate([vf[:, hl] * beta, kb * eg], axis=-1)
        qg = q * eg
        for c in range(nc):
            sl = slice(c * CHUNK, (c + 1) * CHUNK)
            units.append(dict(q=q[sl], k=k[sl], kb=kb[sl], rhs=rhs[sl], qg=qg[sl], g=gcb[sl]))
    for un in units:
        g = un["g"]
        gdiff = g[:, :CHUNK] - g.T[:CHUNK, :]
        un["decay"] = jnp.exp(jnp.where(causal, gdiff, -jnp.inf))
    for un in units:
        un["m"] = jnp.where(strict, _mm_nt(un["kb"], un["k"]) * un["decay"], 0.0)
    for un in units:
        un["tinv"] = eye - un["m"]
        un["pw"] = _mm(un["m"], un["m"])
    for it in range(5):
        for un in units:
            un["tinv"] = un["tinv"] + _mm(un["tinv"], un["pw"])
        if it < 4:
            for un in units:
                un["pw"] = _mm(un["pw"], un["pw"])
    for un in units:
        un["sol"] = _mm(un["tinv"], un["rhs"])
        un["attn"] = _mm_nt(un["q"], un["k"]) * un["decay"]

    gate = gate_ref[...].astype(F32)
    ng = ng_ref[...]
    states = [state_ref[hh] for hh in range(hb)]
    for c in range(nc):
        sl = slice(c * CHUNK, (c + 1) * CHUNK)
        for hh in range(hb):
            un = units[hh * nc + c]
            hl = slice(hh * HEAD_DIM, (hh + 1) * HEAD_DIM)
            u, w = un["sol"][:, :HEAD_DIM], un["sol"][:, HEAD_DIM:]
            g = un["g"]
            g_last = g[CHUNK - 1:CHUNK, :]
            sb = states[hh].astype(BF16)
            v_new = u - jnp.dot(w.astype(BF16), sb, preferred_element_type=F32)
            vnb = v_new.astype(BF16)
            o = (jnp.dot(un["qg"].astype(BF16), sb, preferred_element_type=F32)
                 + jnp.dot(un["attn"].astype(BF16), vnb, preferred_element_type=F32))
            k_dec = un["k"] * jnp.exp(g_last - g)
            states[hh] = states[hh] * jnp.exp(g_last) + jnp.dot(k_dec.T.astype(BF16), vnb,
                                                                preferred_element_type=F32)
            y = o * lax.rsqrt(jnp.mean(o * o, axis=-1, keepdims=True) + RMS_EPS) * ng
            o_ref[sl, hl] = (y * _silu(gate[sl, hl])).astype(o_ref.dtype)
    for hh in range(hb):
        state_ref[hh] = states[hh]


def deltanet(h, gb, conv_w, norm_g, B, S):
    T = h.shape[0]
    rows = min(DN_ROWS, S)
    nr = S // rows
    hb = DN_HB
    width = hb * HEAD_DIM

    def col(c):
        return pl.BlockSpec((rows, width), lambda b, hg, r: (b * nr + r, c // hb + hg))

    def cw(c):
        return pl.BlockSpec((DN_CONV, width), lambda b, hg, r: (0, c // hb + hg))

    return pl.pallas_call(
        functools.partial(_dn_kernel, rows=rows, hb=hb),
        grid=(B, DN_HEADS // hb, nr),
        in_specs=[col(COL_BQ), col(COL_BK), col(COL_BV), col(COL_BG),
                  pl.BlockSpec((rows, LANES), lambda b, hg, r: (b * nr + r, 0)),
                  cw(0), cw(DN_HEADS), cw(2 * DN_HEADS),
                  pl.BlockSpec((1, LANES), lambda b, hg, r: (0, 0))],
        out_specs=pl.BlockSpec((rows, width), lambda b, hg, r: (b * nr + r, hg)),
        out_shape=jax.ShapeDtypeStruct((T, HW), BF16),
        scratch_shapes=[pltpu.VMEM((rows + 8, width), F32), pltpu.VMEM((3, 8, width), F32),
                        pltpu.VMEM((hb, HEAD_DIM, HEAD_DIM), F32)],
        compiler_params=_cparams(("arbitrary", "arbitrary", "arbitrary")),
        name="deltanet",
    )(h, h, h, h, gb, conv_w, conv_w, conv_w, norm_g.reshape(1, LANES))


def _layer_norm(z, g, b):
    mu = jnp.mean(z, axis=-1, keepdims=True)
    var = jnp.mean(jnp.square(z - mu), axis=-1, keepdims=True)
    return (z - mu) * lax.rsqrt(var + LN_EPS) * g + b


def _outproj_kernel(ya_ref, yb_ref, yc_ref, x_ref, wa_ref, wb_ref, wc_ref, g_ref, b_ref, rwh_ref, rwl_ref,
                    rb_ref, x1_ref, xp_ref, rt_ref):
    mixed = (jnp.dot(ya_ref[...], wa_ref[...], preferred_element_type=F32)
             + jnp.dot(yb_ref[...], wb_ref[...], preferred_element_type=F32)
             + jnp.dot(yc_ref[...], wc_ref[...], preferred_element_type=F32))
    x1 = _layer_norm(DEEPNORM_ALPHA * x_ref[...] + mixed, g_ref[...], b_ref[...])
    x1_ref[...] = x1
    _store_planes(xp_ref, _pack_rows(x1))

    x_hi = x1.astype(BF16)
    x_lo = (x1 - x_hi.astype(F32)).astype(BF16)
    logits = (jnp.dot(x_hi, rwh_ref[...], preferred_element_type=F32)
              + jnp.dot(x_lo, rwh_ref[...], preferred_element_type=F32)
              + jnp.dot(x_hi, rwl_ref[...], preferred_element_type=F32))
    scores = jax.nn.sigmoid(logits)
    tm = scores.shape[0]
    lane = lax.broadcasted_iota(jnp.int32, scores.shape, 1)
    lanef = lane.astype(F32)
    neg = -jnp.inf
    big = float(LANES)
    sel = jnp.where(lane < N_EXPERTS, scores + rb_ref[...], neg)

    def top2(mg):
        t1 = jnp.max(mg, axis=-1, keepdims=True)
        i1 = jnp.min(jnp.where(mg == t1, lanef, big), axis=-1, keepdims=True)
        mg2 = jnp.where(lanef == i1, neg, mg)
        t2 = jnp.max(mg2, axis=-1, keepdims=True)
        i2 = jnp.min(jnp.where(mg2 == t2, lanef, big), axis=-1, keepdims=True)
        return t1 + t2, i1, i2

    best_s = best_1 = best_2 = None
    for gidx in range(N_EXPERT_GROUPS):
        in_g = (lane >= gidx * EXPERTS_PER_GROUP) & (lane < (gidx + 1) * EXPERTS_PER_GROUP)
        gs, i1, i2 = top2(jnp.where(in_g, sel, neg))
        if best_s is None:
            best_s, best_1, best_2 = gs, i1, i2
        else:
            take = gs > best_s
            best_s = jnp.where(take, gs, best_s)
            best_1 = jnp.where(take, i1, best_1)
            best_2 = jnp.where(take, i2, best_2)
    s1 = jnp.sum(jnp.where(lanef == best_1, scores, 0.0), axis=-1, keepdims=True)
    s2 = jnp.sum(jnp.where(lanef == best_2, scores, 0.0), axis=-1, keepdims=True)
    tot = s1 + s2
    l8 = lax.broadcasted_iota(jnp.int32, (tm, 8), 1)
    rt_ref[...] = jnp.where(l8 == 0, best_1, jnp.where(l8 == 1, best_2, jnp.where(
        l8 == 2, s1 / tot, jnp.where(l8 == 3, s2 / tot, 0.0))))


def out_proj_ln_router(ya, yb, yc, x, w_out, ln_g, ln_b, rw_pad, rb_pad):
    T = x.shape[0]
    tm = min(256, T)
    wa = w_out[:SGU_WIDTH].astype(BF16)
    wb = w_out[SGU_WIDTH:SGU_WIDTH + HW].astype(BF16)
    wc = w_out[SGU_WIDTH + HW:].astype(BF16)
    rw_hi = rw_pad.astype(BF16)
    rw_lo = (rw_pad - rw_hi.astype(F32)).astype(BF16)

    def rowblk(width):
        return pl.BlockSpec((tm, width), lambda i: (i, 0))

    def full(shape):
        return pl.BlockSpec(shape, lambda i: (0, 0))

    return pl.pallas_call(
        _outproj_kernel,
        grid=(T // tm,),
        in_specs=[rowblk(SGU_WIDTH), rowblk(HW), rowblk(HW), rowblk(D_MODEL),
                  full((SGU_WIDTH, D_MODEL)), full((HW, D_MODEL)), full((HW, D_MODEL)),
                  full((1, D_MODEL)), full((1, D_MODEL)), full((D_MODEL, LANES)), full((D_MODEL, LANES)),
                  full((1, LANES))],
        out_specs=[rowblk(D_MODEL), pl.BlockSpec((N_PLANES, tm, PLANE_W), lambda i: (0, i, 0)), rowblk(8)],
        out_shape=[jax.ShapeDtypeStruct((T, D_MODEL), F32),
                   jax.ShapeDtypeStruct((N_PLANES, T, PLANE_W), jnp.uint32), jax.ShapeDtypeStruct((T, 8), F32)],
        compiler_params=_cparams(("arbitrary",)),
        name="out_proj_ln_router",
    )(ya, yb, yc, x, wa, wb, wc, ln_g.reshape(1, D_MODEL), ln_b.reshape(1, D_MODEL), rw_hi, rw_lo, rb_pad)


def _route_tables(eidx, T, nb):
    A = T * TOP_K
    flat_e = eidx.reshape(A)
    order = jnp.argsort(flat_e, stable=True).astype(jnp.int32)
    experts = jnp.arange(N_EXPERTS, dtype=jnp.int32)
    counts = jnp.sum((flat_e[:, None] == experts[None, :]).astype(jnp.int32), axis=0)
    padded = (counts + MOE_BLK - 1) // MOE_BLK * MOE_BLK
    pad_end = jnp.cumsum(padded)
    pad_start = pad_end - padded
    start = jnp.cumsum(counts) - counts
    pos = jnp.arange(nb * MOE_BLK, dtype=jnp.int32)
    e_of_pos = jnp.sum((pos[:, None] >= pad_end[None, :]).astype(jnp.int32), axis=1)
    onehot = e_of_pos[:, None] == experts[None, :]

    def lookup(tab):
        return jnp.sum(jnp.where(onehot, tab[None, :], 0), axis=1)

    rank = pos - lookup(pad_start)
    valid = rank < lookup(counts)
    a_sorted = order[jnp.clip(lookup(start) + rank, 0, A - 1)]
    src = jnp.where(valid, a_sorted // TOP_K, 0)
    block_expert = jnp.minimum(e_of_pos.reshape(nb, MOE_BLK)[:, 0], N_EXPERTS - 1)
    n_valid = jnp.sum(valid.reshape(nb, MOE_BLK).astype(jnp.int32), axis=1)
    onehot_a = flat_e[:, None] == experts[None, :]
    rank_a = jnp.argsort(order).astype(jnp.int32) - jnp.sum(jnp.where(onehot_a, start[None, :], 0), axis=1)
    pos_a = jnp.sum(jnp.where(onehot_a, pad_start[None, :], 0), axis=1) + rank_a
    inv = pos_a.reshape(T, TOP_K).T.reshape(A)
    return block_expert, n_valid, src, inv


def _moe_kernel(be_ref, nv_ref, x_ref, wg_ref, wu_ref, wd_ref, o_ref):
    n_valid = nv_ref[pl.program_id(0)]

    @pl.when(n_valid > 0)
    def _():
        xb = _unpack_rows(_load_planes(x_ref))
        hg = jnp.dot(xb, wg_ref[...], preferred_element_type=F32)
        hu = jnp.dot(xb, wu_ref[...], preferred_element_type=F32)
        hb = _silu(hg) * hu
        _store_planes(o_ref, _pack_rows(jnp.dot(hb, wd_ref[...], preferred_element_type=F32)))

    @pl.when(n_valid == 0)
    def _():
        o_ref[...] = jnp.zeros(o_ref.shape, o_ref.dtype)


def moe(xp, eidx, wg, wu, wd, layer):
    T = xp.shape[1]
    A = T * TOP_K
    nb = -(-A // MOE_BLK) + N_EXPERTS
    block_expert, n_valid, src, inv = _route_tables(eidx, T, nb)
    xs = _gather_planes(xp, src)

    def w_blk(rows, cols):
        return pl.BlockSpec((None, None, rows, cols), lambda b, be, nv: (layer, be[b], 0, 0),
                            pipeline_mode=pl.Buffered(1))

    row_blk = pl.BlockSpec((N_PLANES, MOE_BLK, PLANE_W), lambda b, be, nv: (0, b, 0))
    grid_spec = pltpu.PrefetchScalarGridSpec(
        num_scalar_prefetch=2,
        grid=(nb,),
        in_specs=[row_blk, w_blk(D_MODEL, D_EXPERT), w_blk(D_MODEL, D_EXPERT), w_blk(D_EXPERT, D_MODEL)],
        out_specs=row_blk,
    )
    ys = pl.pallas_call(
        _moe_kernel,
        grid_spec=grid_spec,
        out_shape=jax.ShapeDtypeStruct((N_PLANES, nb * MOE_BLK, PLANE_W), jnp.uint32),
        compiler_params=_cparams(("arbitrary",)),
        name="moe",
    )(block_expert, n_valid, xs, wg, wu, wd)
    return _gather_planes(ys, inv)


def _combine_kernel(x_ref, y0_ref, y1_ref, rt_ref, g_ref, b_ref, o_ref, ob_ref):
    rt = rt_ref[...]
    ffn = _unpack_rows(_load_planes(y0_ref)) * rt[:, 2:3] + _unpack_rows(_load_planes(y1_ref)) * rt[:, 3:4]
    x2 = _layer_norm(DEEPNORM_ALPHA * x_ref[...] + ffn, g_ref[...], b_ref[...])
    o_ref[...] = x2
    ob_ref[...] = x2.astype(BF16)


def combine_ln(x1, y, rt, ln_g, ln_b):
    T = x1.shape[0]
    tm = min(256, T)
    nt = T // tm
    blk = pl.BlockSpec((tm, D_MODEL), lambda i: (i, 0))
    vec = pl.BlockSpec((1, D_MODEL), lambda i: (0, 0))
    return pl.pallas_call(
        _combine_kernel,
        grid=(nt,),
        in_specs=[blk, pl.BlockSpec((N_PLANES, tm, PLANE_W), lambda i: (0, i, 0)),
                  pl.BlockSpec((N_PLANES, tm, PLANE_W), lambda i: (0, nt + i, 0)),
                  pl.BlockSpec((tm, 8), lambda i: (i, 0)), vec, vec],
        out_specs=[blk, blk],
        out_shape=[jax.ShapeDtypeStruct((T, D_MODEL), F32), jax.ShapeDtypeStruct((T, D_MODEL), BF16)],
        compiler_params=_cparams(("arbitrary",)),
        name="combine_ln",
    )(x1, y, y, rt, ln_g.reshape(1, D_MODEL), ln_b.reshape(1, D_MODEL))


def _repack_kernel(sa_ref, sb_ref, cls_ref, a_ref, b_ref, o_ref):
    cls = cls_ref[pl.program_id(0)]
    lane = lax.broadcasted_iota(jnp.int32, a_ref.shape, 1)
    shift = AB_SRC_COL % LANES + 2 * DN_HEADS

    @pl.when(cls == 0)
    def _():
        o_ref[...] = a_ref[...].astype(o_ref.dtype)

    @pl.when(cls == 1)
    def _():
        left = pltpu.roll(a_ref[...], LANES - shift, axis=1)
        right = pltpu.roll(b_ref[...], LANES - shift, axis=1)
        o_ref[...] = jnp.where(lane < LANES - shift, left, right).astype(o_ref.dtype)

    @pl.when(cls == 2)
    def _():
        o_ref[...] = jnp.where(lane < 2 * DN_HEADS, a_ref[...], 0.0).astype(o_ref.dtype)


def repack_w_in(w_in, layer):
    n_aligned = AB_SRC_COL // LANES
    sa = np.array(list(range(n_aligned)) + [n_aligned + j for j in range(COL_AB - n_aligned)] + [n_aligned],
                  np.int32)
    sb = np.minimum(sa + 1, w_in.shape[2] // LANES).astype(np.int32)
    cls = np.array([0] * n_aligned + [1] * (COL_AB - n_aligned) + [2], np.int32)
    rows = w_in.shape[1]
    grid_spec = pltpu.PrefetchScalarGridSpec(
        num_scalar_prefetch=3,
        grid=(NW // LANES,),
        in_specs=[pl.BlockSpec((None, rows, LANES), lambda j, sa, sb, cls: (layer, 0, sa[j])),
                  pl.BlockSpec((None, rows, LANES), lambda j, sa, sb, cls: (layer, 0, sb[j]))],
        out_specs=pl.BlockSpec((rows, LANES), lambda j, sa, sb, cls: (0, j)),
    )
    return pl.pallas_call(
        _repack_kernel,
        grid_spec=grid_spec,
        out_shape=jax.ShapeDtypeStruct((rows, NW), BF16),
        compiler_params=_cparams(("arbitrary",)),
        name="repack_w_in",
    )(jnp.asarray(sa), jnp.asarray(sb), jnp.asarray(cls), w_in, w_in)


def kernel(x, w_in, sgu_norm_g, sgu_norm_b, sgu_w, sgu_b, dn_conv_w, dn_a_log, dn_dt_bias, dn_norm_g,
           diff_lambda_q1, diff_lambda_k1, diff_lambda_q2, diff_lambda_k2, diff_norm_g, w_out, ln1_g, ln1_b,
           router_w, router_bias, moe_w_gate, moe_w_up, moe_w_down, ln2_g, ln2_b):
    B, S, D = x.shape
    T = B * S
    xf = x.reshape(T, D)
    xb = xf.astype(BF16)
    cos_t, sin_t = _rope_tables(S)
    rw_pad = jnp.zeros((D, LANES), F32).at[:, :N_EXPERTS].set(router_w)
    rb_pad = jnp.zeros((1, LANES), F32).at[0, :N_EXPERTS].set(router_bias)
    for l in range(DEPTH):
        h = in_proj(xb, repack_w_in(w_in, l))
        y_a = sgu(h, sgu_norm_g[l], sgu_norm_b[l], sgu_w[l], sgu_b[l])
        y_b = deltanet(h, dn_gates(h, dn_a_log[l], dn_dt_bias[l]), dn_conv_w[l], dn_norm_g[l], B, S)
        qk = rope(h, cos_t, sin_t)
        lam_params = jnp.stack([diff_lambda_q1[l], diff_lambda_k1[l], diff_lambda_q2[l], diff_lambda_k2[l]])
        lambda_init = 0.8 - 0.6 * math.exp(-0.3 * l)
        y_c = diff_attention(qk, vprep(h, B, S), lam_params, diff_norm_g[l], B, S, lambda_init)
        x1, xp, rt = out_proj_ln_router(y_a, y_b, y_c, xf, w_out[l], ln1_g[l], ln1_b[l], rw_pad, rb_pad)
        eidx = rt[:, :TOP_K].astype(jnp.int32)
        y = moe(xp, eidx, moe_w_gate, moe_w_up, moe_w_down, l)
        xf, xb = combine_ln(x1, y, rt, ln2_g[l], ln2_b[l])
    return xf.reshape(B, S, D)
```

```python
import functools
import math

import numpy as np
import jax
import jax.numpy as jnp
from jax import lax
from jax.experimental import pallas as pl
from jax.experimental.pallas import tpu as pltpu
from jax.experimental.pallas import tpu_sc as plsc

F32 = jnp.float32
BF16 = jnp.bfloat16
HIGHEST = lax.Precision.HIGHEST

D_MODEL = 2048
DEPTH = 2
CHUNK = 64
HEAD_DIM = 128
SGU_GROUPS = 4
SGU_WIDTH = SGU_GROUPS * HEAD_DIM
SGU_BLOCK = 128
DN_HEADS = 6
DN_CONV = 4
DIFF_HEADS = 6
DIFF_QK_DIM = 64
ROPE_THETA = 10000.0
N_EXPERTS = 32
N_EXPERT_GROUPS = 4
EXPERTS_PER_GROUP = N_EXPERTS // N_EXPERT_GROUPS
TOP_K = 2
D_EXPERT = D_MODEL // 2
DEEPNORM_ALPHA = (2 * DEPTH) ** 0.25
LN_EPS = 1e-5
RMS_EPS = 1e-6

LANES = 128
VMEM_LIMIT = 56 * 1024 * 1024

HW = DN_HEADS * HEAD_DIM
COL_AU, COL_AV = 0, 4
COL_BQ, COL_BK, COL_BV, COL_BG = 8, 14, 20, 26
COL_CQ, COL_CK, COL_CV = 32, 38, 44
COL_AB = 50
AB_SRC_COL = 2 * SGU_WIDTH + 4 * HW
NW = 51 * LANES
N_TILES_IN = 3

MOE_BLK = 256


def _cparams(sem, vmem=VMEM_LIMIT):
    return pltpu.CompilerParams(dimension_semantics=sem, vmem_limit_bytes=vmem)


HALF_D = D_MODEL // 2


def _pack_rows(x):
    lo = lax.bitcast_convert_type(x[:, :HALF_D].astype(BF16).astype(F32), jnp.uint32)
    hi = lax.bitcast_convert_type(x[:, HALF_D:].astype(BF16).astype(F32), jnp.uint32)
    return (hi & jnp.uint32(0xFFFF0000)) | (lo >> 16)


def _unpack_rows(w):
    lo = lax.bitcast_convert_type(w << 16, F32)
    hi = lax.bitcast_convert_type(w & jnp.uint32(0xFFFF0000), F32)
    return jnp.concatenate([lo, hi], axis=1)


N_PLANES = 4
PLANE_W = HALF_D // N_PLANES


def _store_planes(ref, packed):
    for j in range(N_PLANES):
        ref[j] = packed[:, j * PLANE_W:(j + 1) * PLANE_W]


def _load_planes(ref):
    return jnp.concatenate([ref[j] for j in range(N_PLANES)], axis=1)


SC_WINDOW = 128


def sc_gather(x, idx):
    M = idx.shape[0]
    W = x.shape[1]
    mesh = plsc.VectorSubcoreMesh(core_axis_name="core", subcore_axis_name="subcore")

    @functools.partial(pl.kernel, out_type=jax.ShapeDtypeStruct((M, W), x.dtype), mesh=mesh, scratch_types=[])
    def gather_kernel(x_hbm, i_hbm, o_hbm):
        def body(i_vmem, o_vmem):
            pltpu.sync_copy(x_hbm.at[i_vmem.at[0]], o_vmem)

        pltpu.emit_pipeline(
            body,
            grid=(M // SC_WINDOW,),
            in_specs=[pl.BlockSpec((1, SC_WINDOW), index_map=lambda i: (0, i))],
            out_specs=[pl.BlockSpec((SC_WINDOW, W), index_map=lambda i: (i, 0))],
            core_axis_name=("core", "subcore"),
            dimension_semantics=(pltpu.PARALLEL,),
        )(i_hbm, o_hbm)

    return gather_kernel(x, idx.reshape(1, M))


def _gather_planes(x, idx):
    n = x.shape[1]
    flat_idx = (jnp.arange(N_PLANES, dtype=jnp.int32)[:, None] * n + idx[None, :]).reshape(-1)
    return sc_gather(x.reshape(N_PLANES * n, PLANE_W), flat_idx).reshape(N_PLANES, idx.shape[0], PLANE_W)


def _matmul_kernel(x_ref, w_ref, o_ref):
    o_ref[...] = jnp.dot(x_ref[...], w_ref[...], preferred_element_type=F32).astype(o_ref.dtype)


def in_proj(xb, w):
    T = xb.shape[0]
    tm = min(512, T)
    tn = NW // N_TILES_IN
    return pl.pallas_call(
        _matmul_kernel,
        grid=(N_TILES_IN, T // tm),
        in_specs=[pl.BlockSpec((tm, D_MODEL), lambda j, i: (i, 0)),
                  pl.BlockSpec((D_MODEL, tn), lambda j, i: (0, j))],
        out_specs=pl.BlockSpec((tm, tn), lambda j, i: (i, j)),
        out_shape=jax.ShapeDtypeStruct((T, NW), BF16),
        compiler_params=_cparams(("arbitrary", "arbitrary")),
        name="in_proj",
    )(xb, w)


def _gelu(t):
    return t * (lax.erf(t * (2.0 ** -0.5)) + 1.0) * 0.5


def _sgu_kernel(u_ref, v_ref, ng_ref, nb_ref, w_ref, b_ref, o_ref, *, rows):
    u = _gelu(u_ref[...].astype(F32))
    v = _gelu(v_ref[...].astype(F32))
    mu = jnp.mean(v, axis=-1, keepdims=True)
    var = jnp.mean(jnp.square(v - mu), axis=-1, keepdims=True)
    vn = ((v - mu) * lax.rsqrt(var + LN_EPS) * ng_ref[...] + nb_ref[...]).astype(BF16)
    ii = lax.broadcasted_iota(jnp.int32, (SGU_BLOCK, SGU_BLOCK), 0)
    jj = lax.broadcasted_iota(jnp.int32, (SGU_BLOCK, SGU_BLOCK), 1)
    w = jnp.where(jj // CHUNK <= ii // CHUNK, w_ref[...], 0.0).astype(BF16)
    bias = b_ref[...]
    for n in range(rows // SGU_BLOCK):
        sl = slice(n * SGU_BLOCK, (n + 1) * SGU_BLOCK)
        s = jnp.dot(w, vn[sl], preferred_element_type=F32) + bias
        o_ref[sl, :] = (u[sl] * s).astype(o_ref.dtype)


def sgu(h, norm_g, norm_b, w_s, b_s):
    T = h.shape[0]
    rows = min(1024, T)
    return pl.pallas_call(
        functools.partial(_sgu_kernel, rows=rows),
        grid=(T // rows, SGU_GROUPS),
        in_specs=[pl.BlockSpec((rows, LANES), lambda i, g: (i, COL_AU + g)),
                  pl.BlockSpec((rows, LANES), lambda i, g: (i, COL_AV + g)),
                  pl.BlockSpec((1, LANES), lambda i, g: (0, g)),
                  pl.BlockSpec((1, LANES), lambda i, g: (0, g)),
                  pl.BlockSpec((None, SGU_BLOCK, SGU_BLOCK), lambda i, g: (g, 0, 0)),
                  pl.BlockSpec((None, SGU_BLOCK, 1), lambda i, g: (g, 0, 0))],
        out_specs=pl.BlockSpec((rows, LANES), lambda i, g: (i, g)),
        out_shape=jax.ShapeDtypeStruct((T, SGU_WIDTH), BF16),
        compiler_params=_cparams(("arbitrary", "arbitrary")),
        name="sgu",
    )(h, h, norm_g.reshape(1, SGU_WIDTH), norm_b.reshape(1, SGU_WIDTH), w_s,
      b_s.reshape(SGU_GROUPS, SGU_BLOCK, 1))


def _rope_kernel(x_ref, cos_ref, sin_ref, o_ref):
    half = DIFF_QK_DIM // 2
    scale = jnp.where(pl.program_id(1) < DIFF_HEADS // ROPE_HEADS, DIFF_QK_DIM ** -0.5 * math.log2(math.e), 1.0)
    cos = cos_ref[...]
    sin = sin_ref[...]
    lane = lax.broadcasted_iota(jnp.int32, cos.shape, 1)
    for hh in range(ROPE_HEADS):
        hl = slice(hh * LANES, (hh + 1) * LANES)
        x = x_ref[:, hl].astype(F32)
        partner = jnp.where(lane % DIFF_QK_DIM < half,
                            pltpu.roll(x, LANES - half, axis=1), pltpu.roll(x, half, axis=1))
        o_ref[:, hl] = ((x * cos + partner * sin) * scale).astype(o_ref.dtype)


def _rope_tables(S):
    half = DIFF_QK_DIM // 2
    inv_freq = ROPE_THETA ** (-jnp.arange(half, dtype=F32) / half)
    ang = jnp.arange(S, dtype=F32)[:, None] * inv_freq[None, :]
    cos, sin = jnp.cos(ang), jnp.sin(ang)
    cos_t = jnp.tile(cos, (1, LANES // half))
    sin_t = jnp.tile(jnp.concatenate([-sin, sin], axis=1), (1, LANES // DIFF_QK_DIM))
    return cos_t, sin_t


ROPE_HEADS = 2
ROPE_ROWS = 1024


def rope(h, cos_t, sin_t):
    T = h.shape[0]
    S = cos_t.shape[0]
    rows = min(ROPE_ROWS, S)
    ns = S // rows
    width = ROPE_HEADS * LANES
    return pl.pallas_call(
        _rope_kernel,
        grid=(T // rows, 2 * DIFF_HEADS // ROPE_HEADS),
        in_specs=[pl.BlockSpec((rows, width), lambda i, j: (i, COL_CQ // ROPE_HEADS + j)),
                  pl.BlockSpec((rows, LANES), lambda i, j: (i % ns, 0)),
                  pl.BlockSpec((rows, LANES), lambda i, j: (i % ns, 0))],
        out_specs=pl.BlockSpec((rows, width), lambda i, j: (i, j)),
        out_shape=jax.ShapeDtypeStruct((T, 2 * HW), BF16),
        compiler_params=_cparams(("arbitrary", "arbitrary")),
        name="rope",
    )(h, cos_t, sin_t)


ATT_TILE = 1024
VT_ROWS = HEAD_DIM + 16


def _vprep_kernel(v_ref, o_ref):
    vt = v_ref[...].astype(F32).T
    o_ref[0:HEAD_DIM, :] = vt.astype(o_ref.dtype)
    row = lax.broadcasted_iota(jnp.int32, (VT_ROWS - HEAD_DIM, vt.shape[1]), 0)
    o_ref[HEAD_DIM:VT_ROWS, :] = jnp.where(row == 0, 1.0, 0.0).astype(o_ref.dtype)


def vprep(h, B, S):
    tk = min(ATT_TILE, S)
    nk = S // tk
    return pl.pallas_call(
        _vprep_kernel,
        grid=(B, DIFF_HEADS, nk),
        in_specs=[pl.BlockSpec((tk, LANES), lambda b, hd, j: (b * nk + j, COL_CV + hd))],
        out_specs=pl.BlockSpec((None, None, VT_ROWS, tk), lambda b, hd, j: (b * DIFF_HEADS + hd, j, 0, 0)),
        out_shape=jax.ShapeDtypeStruct((B * DIFF_HEADS, nk, VT_ROWS, tk), BF16),
        compiler_params=_cparams(("arbitrary", "arbitrary", "arbitrary")),
        name="vprep",
    )(h)


def _attn_kernel(q_ref, k_ref, vt_ref, lam_ref, g_ref, o_ref, acc_ref, *, tq, lambda_init):
    i = pl.program_id(2)
    q = q_ref[...]
    lane = lax.broadcasted_iota(jnp.int32, q.shape, 1)
    zero = jnp.zeros_like(q)
    qs = (jnp.where(lane < DIFF_QK_DIM, q, zero), jnp.where(lane >= DIFF_QK_DIM, q, zero))
    acc_ref[...] = jnp.zeros(acc_ref.shape, F32)
    neg = jnp.full((1, tq), -jnp.inf, F32)

    def block(j, ms, masked):
        kb = k_ref[pl.ds(pl.multiple_of(j * tq, tq), tq), :]
        vb = vt_ref[j]
        s = [lax.dot_general(kb, qs[t], (((1,), (1,)), ((), ())), preferred_element_type=F32)
             for t in range(2)]
        if masked:
            kr = lax.broadcasted_iota(jnp.int32, (tq, tq), 0)
            qc = lax.broadcasted_iota(jnp.int32, (tq, tq), 1)
            allowed = kr // CHUNK <= qc // CHUNK
        new = []
        for t in range(2):
            st = jnp.where(allowed, s[t], -jnp.inf) if masked else s[t]
            m_new = jnp.maximum(ms[t], jnp.max(st, axis=0, keepdims=True))
            alpha = jnp.exp2(ms[t] - m_new)
            p = jnp.exp2(st - m_new).astype(BF16)
            acc_ref[t] = alpha * acc_ref[t] + jnp.dot(vb, p, preferred_element_type=F32)
            new.append(m_new)
        return tuple(new)

    ms = lax.fori_loop(0, i, lambda j, ms: block(j, ms, False), (neg, neg))
    block(i, ms, True)

    lam_v = lam_ref[...]
    lam = (jnp.exp(jnp.sum(lam_v[0:1] * lam_v[1:2])) - jnp.exp(jnp.sum(lam_v[2:3] * lam_v[3:4]))
           + lambda_init)
    a0 = acc_ref[0]
    a1 = acc_ref[1]
    o = a0[:HEAD_DIM] / a0[HEAD_DIM:HEAD_DIM + 1] - lam * (a1[:HEAD_DIM] / a1[HEAD_DIM:HEAD_DIM + 1])
    y = o * lax.rsqrt(jnp.mean(o * o, axis=0, keepdims=True) + RMS_EPS) * g_ref[...]
    o_ref[...] = (y * (1.0 - lambda_init)).T.astype(o_ref.dtype)


def diff_attention(qk, vt, lam_params, norm_g, B, S, lambda_init):
    T = qk.shape[0]
    tq = min(ATT_TILE, S)
    nq = S // tq
    return pl.pallas_call(
        functools.partial(_attn_kernel, tq=tq, lambda_init=lambda_init),
        grid=(B, DIFF_HEADS, nq),
        in_specs=[pl.BlockSpec((tq, LANES), lambda b, hd, i: (b * nq + i, hd)),
                  pl.BlockSpec((S, LANES), lambda b, hd, i: (b, DIFF_HEADS + hd)),
                  pl.BlockSpec((None, nq, VT_ROWS, tq), lambda b, hd, i: (b * DIFF_HEADS + hd, 0, 0, 0)),
                  pl.BlockSpec((4, DIFF_QK_DIM), lambda b, hd, i: (0, 0)),
                  pl.BlockSpec((HEAD_DIM, 1), lambda b, hd, i: (0, 0))],
        out_specs=pl.BlockSpec((tq, LANES), lambda b, hd, i: (b * nq + i, hd)),
        out_shape=jax.ShapeDtypeStruct((T, HW), BF16),
        scratch_shapes=[pltpu.VMEM((2, VT_ROWS, tq), F32)],
        compiler_params=_cparams(("arbitrary", "arbitrary", "arbitrary")),
        name="diff_attn",
    )(qk, qk, vt, lam_params, norm_g.reshape(HEAD_DIM, 1))


def _silu(t):
    return t * jax.nn.sigmoid(t)


def _softplus(t):
    return jnp.maximum(t, 0.0) + jnp.log1p(jnp.exp(-jnp.abs(t)))


def _dot32(a, b):
    return jnp.dot(a, b, preferred_element_type=F32, precision=HIGHEST)


def _mm(a, b):
    return jnp.dot(a.astype(BF16), b.astype(BF16), preferred_element_type=F32)


def _mm_nt(a, b):
    return lax.dot_general(a.astype(BF16), b.astype(BF16), (((1,), (1,)), ((), ())),
                           preferred_element_type=F32)


DN_ROWS = 512
DN_HB = 2


def _gates_kernel(ab_ref, alog_ref, dt_ref, o_ref, *, rows):
    ab = ab_ref[...].astype(F32)
    g_all = -jnp.exp(alog_ref[...]) * _softplus(ab + dt_ref[...])
    beta_all = jax.nn.sigmoid(ab)
    ri = lax.broadcasted_iota(jnp.int32, (CHUNK, CHUNK), 0)
    ci = lax.broadcasted_iota(jnp.int32, (CHUNK, CHUNK), 1)
    tri = (ri >= ci).astype(F32)
    is_decay_lane = lax.broadcasted_iota(jnp.int32, (CHUNK, LANES), 1) < DN_HEADS
    for c in range(rows // CHUNK):
        sl = slice(c * CHUNK, (c + 1) * CHUNK)
        o_ref[sl, :] = jnp.where(is_decay_lane, _dot32(tri, g_all[sl]), beta_all[sl])


def dn_gates(h, a_log, dt_bias):
    T = h.shape[0]
    rows = min(512, T)
    alog_row = jnp.zeros((1, LANES), F32).at[0, :DN_HEADS].set(a_log)
    dt_row = jnp.zeros((1, LANES), F32).at[0, :DN_HEADS].set(dt_bias)
    row = pl.BlockSpec((1, LANES), lambda i: (0, 0))
    return pl.pallas_call(
        functools.partial(_gates_kernel, rows=rows),
        grid=(T // rows,),
        in_specs=[pl.BlockSpec((rows, LANES), lambda i: (i, COL_AB)), row, row],
        out_specs=pl.BlockSpec((rows, LANES), lambda i: (i, 0)),
        out_shape=jax.ShapeDtypeStruct((T, LANES), F32),
        compiler_params=_cparams(("arbitrary",)),
        name="dn_gates",
    )(h, alog_row, dt_row)


def _dn_kernel(q_ref, k_ref, v_ref, gate_ref, gb_ref, cwq_ref, cwk_ref, cwv_ref, ng_ref, o_ref,
               ext_ref, tail_ref, state_ref, *, rows, hb):
    hd0 = pl.program_id(1) * hb
    r = pl.program_id(2)
    pad = 8
    nc = rows // CHUNK

    @pl.when(r == 0)
    def _():
        tail_ref[...] = jnp.zeros(tail_ref.shape, F32)
        state_ref[...] = jnp.zeros(state_ref.shape, F32)

    def conv_silu(x_ref, cw_ref, slot):
        x = x_ref[...].astype(F32)
        ext_ref[0:pad, :] = tail_ref[slot]
        ext_ref[pad:pad + rows, :] = x
        tail_ref[slot] = x[rows - pad:rows]
        cw = cw_ref[...]
        acc = None
        for t in range(DN_CONV):
            off = pad - (DN_CONV - 1) + t
            term = ext_ref[off:off + rows, :] * cw[t:t + 1, :]
            acc = term if acc is None else acc + term
        return _silu(acc)

    def l2n(t):
        return t * lax.rsqrt(jnp.sum(t * t, axis=-1, keepdims=True) + RMS_EPS)

    qf = conv_silu(q_ref, cwq_ref, 0)
    kf = conv_silu(k_ref, cwk_ref, 1)
    vf = conv_silu(v_ref, cwv_ref, 2)
    gb = gb_ref[...]
    lane = lax.broadcasted_iota(jnp.int32, gb.shape, 1)

    ri = lax.broadcasted_iota(jnp.int32, (CHUNK, CHUNK), 0)
    ci = lax.broadcasted_iota(jnp.int32, (CHUNK, CHUNK), 1)
    causal = ri >= ci
    strict = ri > ci
    eye = (ri == ci).astype(F32)

    units = []
    for hh in range(hb):
        hl = slice(hh * HEAD_DIM, (hh + 1) * HEAD_DIM)
        q = l2n(qf[:, hl]) * (HEAD_DIM ** -0.5)
        k = l2n(kf[:, hl])
        gc_col = jnp.sum(jnp.where(lane == hd0 + hh, gb, 0.0), axis=-1, keepdims=True)
        beta = jnp.sum(jnp.where(lane == hd0 + hh + DN_HEADS, gb, 0.0), axis=-1, keepdims=True)
        gcb = jnp.broadcast_to(gc_col, (rows, HEAD_DIM))
        eg = jnp.exp(gcb)
        kb = k * beta
        rhs = jnp.concatenate([vf[:, hl] * beta, kb * eg], axis=-1)
        qg = q * eg
        for c in range(nc):
            sl = slice(c * CHUNK, (c + 1) * CHUNK)
            units.append(dict(q=q[sl], k=k[sl], kb=kb[sl], rhs=rhs[sl], qg=qg[sl], g=gcb[sl]))
    for un in units:
        g = un["g"]
        gdiff = g[:, :CHUNK] - g.T[:CHUNK, :]
        un["decay"] = jnp.exp(jnp.where(causal, gdiff, -jnp.inf))
    for un in units:
        un["m"] = jnp.where(strict, _mm_nt(un["kb"], un["k"]) * un["decay"], 0.0)
    for un in units:
        un["tinv"] = eye - un["m"]
        un["pw"] = _mm(un["m"], un["m"])
    for it in range(5):
        for un in units:
            un["tinv"] = un["tinv"] + _mm(un["tinv"], un["pw"])
        if it < 4:
            for un in units:
                un["pw"] = _mm(un["pw"], un["pw"])
    for un in units:
        un["sol"] = _mm(un["tinv"], un["rhs"])
        un["attn"] = _mm_nt(un["q"], un["k"]) * un["decay"]

    gate = gate_ref[...].astype(F32)
    ng = ng_ref[...]
    states = [state_ref[hh] for hh in range(hb)]
    for c in range(nc):
        sl = slice(c * CHUNK, (c + 1) * CHUNK)
        for hh in range(hb):
            un = units[hh * nc + c]
            hl = slice(hh * HEAD_DIM, (hh + 1) * HEAD_DIM)
            u, w = un["sol"][:, :HEAD_DIM], un["sol"][:, HEAD_DIM:]
            g = un["g"]
            g_last = g[CHUNK - 1:CHUNK, :]
            sb = states[hh].astype(BF16)
            v_new = u - jnp.dot(w.astype(BF16), sb, preferred_element_type=F32)
            vnb = v_new.astype(BF16)
            o = (jnp.dot(un["qg"].astype(BF16), sb, preferred_element_type=F32)
                 + jnp.dot(un["attn"].astype(BF16), vnb, preferred_element_type=F32))
            k_dec = un["k"] * jnp.exp(g_last - g)
            states[hh] = states[hh] * jnp.exp(g_last) + jnp.dot(k_dec.T.astype(BF16), vnb,
                                                                preferred_element_type=F32)
            y = o * lax.rsqrt(jnp.mean(o * o, axis=-1, keepdims=True) + RMS_EPS) * ng
            o_ref[sl, hl] = (y * _silu(gate[sl, hl])).astype(o_ref.dtype)
    for hh in range(hb):
        state_ref[hh] = states[hh]


def deltanet(h, gb, conv_w, norm_g, B, S):
    T = h.shape[0]
    rows = min(DN_ROWS, S)
    nr = S // rows
    hb = DN_HB
    width = hb * HEAD_DIM

    def col(c):
        return pl.BlockSpec((rows, width), lambda b, hg, r: (b * nr + r, c // hb + hg))

    def cw(c):
        return pl.BlockSpec((DN_CONV, width), lambda b, hg, r: (0, c // hb + hg))

    return pl.pallas_call(
        functools.partial(_dn_kernel, rows=rows, hb=hb),
        grid=(B, DN_HEADS // hb, nr),
        in_specs=[col(COL_BQ), col(COL_BK), col(COL_BV), col(COL_BG),
                  pl.BlockSpec((rows, LANES), lambda b, hg, r: (b * nr + r, 0)),
                  cw(0), cw(DN_HEADS), cw(2 * DN_HEADS),
                  pl.BlockSpec((1, LANES), lambda b, hg, r: (0, 0))],
        out_specs=pl.BlockSpec((rows, width), lambda b, hg, r: (b * nr + r, hg)),
        out_shape=jax.ShapeDtypeStruct((T, HW), BF16),
        scratch_shapes=[pltpu.VMEM((rows + 8, width), F32), pltpu.VMEM((3, 8, width), F32),
                        pltpu.VMEM((hb, HEAD_DIM, HEAD_DIM), F32)],
        compiler_params=_cparams(("arbitrary", "arbitrary", "arbitrary")),
        name="deltanet",
    )(h, h, h, h, gb, conv_w, conv_w, conv_w, norm_g.reshape(1, LANES))


def _layer_norm(z, g, b):
    mu = jnp.mean(z, axis=-1, keepdims=True)
    var = jnp.mean(jnp.square(z - mu), axis=-1, keepdims=True)
    return (z - mu) * lax.rsqrt(var + LN_EPS) * g + b


def _outproj_kernel(ya_ref, yb_ref, yc_ref, x_ref, wa_ref, wb_ref, wc_ref, g_ref, b_ref, rwh_ref, rwl_ref,
                    rb_ref, x1_ref, xp_ref, rt_ref):
    mixed = (jnp.dot(ya_ref[...], wa_ref[...], preferred_element_type=F32)
             + jnp.dot(yb_ref[...], wb_ref[...], preferred_element_type=F32)
             + jnp.dot(yc_ref[...], wc_ref[...], preferred_element_type=F32))
    x1 = _layer_norm(DEEPNORM_ALPHA * x_ref[...] + mixed, g_ref[...], b_ref[...])
    x1_ref[...] = x1
    _store_planes(xp_ref, _pack_rows(x1))

    x_hi = x1.astype(BF16)
    x_lo = (x1 - x_hi.astype(F32)).astype(BF16)
    logits = (jnp.dot(x_hi, rwh_ref[...], preferred_element_type=F32)
              + jnp.dot(x_lo, rwh_ref[...], preferred_element_type=F32)
              + jnp.dot(x_hi, rwl_ref[...], preferred_element_type=F32))
    scores = jax.nn.sigmoid(logits)
    tm = scores.shape[0]
    lane = lax.broadcasted_iota(jnp.int32, scores.shape, 1)
    lanef = lane.astype(F32)
    neg = -jnp.inf
    big = float(LANES)
    sel = jnp.where(lane < N_EXPERTS, scores + rb_ref[...], neg)

    def top2(mg):
        t1 = jnp.max(mg, axis=-1, keepdims=True)
        i1 = jnp.min(jnp.where(mg == t1, lanef, big), axis=-1, keepdims=True)
        mg2 = jnp.where(lanef == i1, neg, mg)
        t2 = jnp.max(mg2, axis=-1, keepdims=True)
        i2 = jnp.min(jnp.where(mg2 == t2, lanef, big), axis=-1, keepdims=True)
        return t1 + t2, i1, i2

    best_s = best_1 = best_2 = None
    for gidx in range(N_EXPERT_GROUPS):
        in_g = (lane >= gidx * EXPERTS_PER_GROUP) & (lane < (gidx + 1) * EXPERTS_PER_GROUP)
        gs, i1, i2 = top2(jnp.where(in_g, sel, neg))
        if best_s is None:
            best_s, best_1, best_2 = gs, i1, i2
        else:
            take = gs > best_s
            best_s = jnp.where(take, gs, best_s)
            best_1 = jnp.where(take, i1, best_1)
            best_2 = jnp.where(take, i2, best_2)
    s1 = jnp.sum(jnp.where(lanef == best_1, scores, 0.0), axis=-1, keepdims=True)
    s2 = jnp.sum(jnp.where(lanef == best_2, scores, 0.0), axis=-1, keepdims=True)
    tot = s1 + s2
    l8 = lax.broadcasted_iota(jnp.int32, (tm, 8), 1)
    rt_ref[...] = jnp.where(l8 == 0, best_1, jnp.where(l8 == 1, best_2, jnp.where(
        l8 == 2, s1 / tot, jnp.where(l8 == 3, s2 / tot, 0.0))))


def out_proj_ln_router(ya, yb, yc, x, w_out, ln_g, ln_b, rw_pad, rb_pad):
    T = x.shape[0]
    tm = min(256, T)
    wa = w_out[:SGU_WIDTH].astype(BF16)
    wb = w_out[SGU_WIDTH:SGU_WIDTH + HW].astype(BF16)
    wc = w_out[SGU_WIDTH + HW:].astype(BF16)
    rw_hi = rw_pad.astype(BF16)
    rw_lo = (rw_pad - rw_hi.astype(F32)).astype(BF16)

    def rowblk(width):
        return pl.BlockSpec((tm, width), lambda i: (i, 0))

    def full(shape):
        return pl.BlockSpec(shape, lambda i: (0, 0))

    return pl.pallas_call(
        _outproj_kernel,
        grid=(T // tm,),
        in_specs=[rowblk(SGU_WIDTH), rowblk(HW), rowblk(HW), rowblk(D_MODEL),
                  full((SGU_WIDTH, D_MODEL)), full((HW, D_MODEL)), full((HW, D_MODEL)),
                  full((1, D_MODEL)), full((1, D_MODEL)), full((D_MODEL, LANES)), full((D_MODEL, LANES)),
                  full((1, LANES))],
        out_specs=[rowblk(D_MODEL), pl.BlockSpec((N_PLANES, tm, PLANE_W), lambda i: (0, i, 0)), rowblk(8)],
        out_shape=[jax.ShapeDtypeStruct((T, D_MODEL), F32),
                   jax.ShapeDtypeStruct((N_PLANES, T, PLANE_W), jnp.uint32), jax.ShapeDtypeStruct((T, 8), F32)],
        compiler_params=_cparams(("arbitrary",)),
        name="out_proj_ln_router",
    )(ya, yb, yc, x, wa, wb, wc, ln_g.reshape(1, D_MODEL), ln_b.reshape(1, D_MODEL), rw_hi, rw_lo, rb_pad)


def _route_tables(eidx, T, nb):
    A = T * TOP_K
    flat_e = eidx.reshape(A)
    order = jnp.argsort(flat_e, stable=True).astype(jnp.int32)
    experts = jnp.arange(N_EXPERTS, dtype=jnp.int32)
    counts = jnp.sum((flat_e[:, None] == experts[None, :]).astype(jnp.int32), axis=0)
    padded = (counts + MOE_BLK - 1) // MOE_BLK * MOE_BLK
    pad_end = jnp.cumsum(padded)
    pad_start = pad_end - padded
    start = jnp.cumsum(counts) - counts
    pos = jnp.arange(nb * MOE_BLK, dtype=jnp.int32)
    e_of_pos = jnp.sum((pos[:, None] >= pad_end[None, :]).astype(jnp.int32), axis=1)
    onehot = e_of_pos[:, None] == experts[None, :]

    def lookup(tab):
        return jnp.sum(jnp.where(onehot, tab[None, :], 0), axis=1)

    rank = pos - lookup(pad_start)
    valid = rank < lookup(counts)
    a_sorted = order[jnp.clip(lookup(start) + rank, 0, A - 1)]
    src = jnp.where(valid, a_sorted // TOP_K, pos % T)
    block_expert = jnp.minimum(e_of_pos.reshape(nb, MOE_BLK)[:, 0], N_EXPERTS - 1)
    n_valid = jnp.sum(valid.reshape(nb, MOE_BLK).astype(jnp.int32), axis=1)
    onehot_a = flat_e[:, None] == experts[None, :]
    rank_a = jnp.argsort(order).astype(jnp.int32) - jnp.sum(jnp.where(onehot_a, start[None, :], 0), axis=1)
    pos_a = jnp.sum(jnp.where(onehot_a, pad_start[None, :], 0), axis=1) + rank_a
    inv = pos_a.reshape(T, TOP_K).T.reshape(A)
    return block_expert, n_valid, src, inv


def _moe_kernel(be_ref, nv_ref, x_ref, wg_ref, wu_ref, wd_ref, o_ref):
    n_valid = nv_ref[pl.program_id(0)]

    @pl.when(n_valid > 0)
    def _():
        xb = _unpack_rows(_load_planes(x_ref))
        hg = jnp.dot(xb, wg_ref[...], preferred_element_type=F32)
        hu = jnp.dot(xb, wu_ref[...], preferred_element_type=F32)
        hb = _silu(hg) * hu
        _store_planes(o_ref, _pack_rows(jnp.dot(hb, wd_ref[...], preferred_element_type=F32)))

    @pl.when(n_valid == 0)
    def _():
        o_ref[...] = jnp.zeros(o_ref.shape, o_ref.dtype)


def moe(xp, eidx, wg, wu, wd, layer):
    T = xp.shape[1]
    A = T * TOP_K
    nb = -(-A // MOE_BLK) + N_EXPERTS
    block_expert, n_valid, src, inv = _route_tables(eidx, T, nb)
    xs = _gather_planes(xp, src)

    def w_blk(rows, cols):
        return pl.BlockSpec((None, None, rows, cols), lambda b, be, nv: (layer, be[b], 0, 0),
                            pipeline_mode=pl.Buffered(1))

    row_blk = pl.BlockSpec((N_PLANES, MOE_BLK, PLANE_W), lambda b, be, nv: (0, b, 0))
    grid_spec = pltpu.PrefetchScalarGridSpec(
        num_scalar_prefetch=2,
        grid=(nb,),
        in_specs=[row_blk, w_blk(D_MODEL, D_EXPERT), w_blk(D_MODEL, D_EXPERT), w_blk(D_EXPERT, D_MODEL)],
        out_specs=row_blk,
    )
    ys = pl.pallas_call(
        _moe_kernel,
        grid_spec=grid_spec,
        out_shape=jax.ShapeDtypeStruct((N_PLANES, nb * MOE_BLK, PLANE_W), jnp.uint32),
        compiler_params=_cparams(("arbitrary",)),
        name="moe",
    )(block_expert, n_valid, xs, wg, wu, wd)
    return _gather_planes(ys, inv)


def _combine_kernel(x_ref, y0_ref, y1_ref, rt_ref, g_ref, b_ref, o_ref, ob_ref):
    rt = rt_ref[...]
    ffn = _unpack_rows(_load_planes(y0_ref)) * rt[:, 2:3] + _unpack_rows(_load_planes(y1_ref)) * rt[:, 3:4]
    x2 = _layer_norm(DEEPNORM_ALPHA * x_ref[...] + ffn, g_ref[...], b_ref[...])
    o_ref[...] = x2
    ob_ref[...] = x2.astype(BF16)


def combine_ln(x1, y, rt, ln_g, ln_b):
    T = x1.shape[0]
    tm = min(256, T)
    nt = T // tm
    blk = pl.BlockSpec((tm, D_MODEL), lambda i: (i, 0))
    vec = pl.BlockSpec((1, D_MODEL), lambda i: (0, 0))
    return pl.pallas_call(
        _combine_kernel,
        grid=(nt,),
        in_specs=[blk, pl.BlockSpec((N_PLANES, tm, PLANE_W), lambda i: (0, i, 0)),
                  pl.BlockSpec((N_PLANES, tm, PLANE_W), lambda i: (0, nt + i, 0)),
                  pl.BlockSpec((tm, 8), lambda i: (i, 0)), vec, vec],
        out_specs=[blk, blk],
        out_shape=[jax.ShapeDtypeStruct((T, D_MODEL), F32), jax.ShapeDtypeStruct((T, D_MODEL), BF16)],
        compiler_params=_cparams(("arbitrary",)),
        name="combine_ln",
    )(x1, y, y, rt, ln_g.reshape(1, D_MODEL), ln_b.reshape(1, D_MODEL))


def _repack_kernel(sa_ref, sb_ref, cls_ref, a_ref, b_ref, o_ref):
    cls = cls_ref[pl.program_id(0)]
    lane = lax.broadcasted_iota(jnp.int32, a_ref.shape, 1)
    shift = AB_SRC_COL % LANES + 2 * DN_HEADS

    @pl.when(cls == 0)
    def _():
        o_ref[...] = a_ref[...].astype(o_ref.dtype)

    @pl.when(cls == 1)
    def _():
        left = pltpu.roll(a_ref[...], LANES - shift, axis=1)
        right = pltpu.roll(b_ref[...], LANES - shift, axis=1)
        o_ref[...] = jnp.where(lane < LANES - shift, left, right).astype(o_ref.dtype)

    @pl.when(cls == 2)
    def _():
        o_ref[...] = jnp.where(lane < 2 * DN_HEADS, a_ref[...], 0.0).astype(o_ref.dtype)


def repack_w_in(w_in, layer):
    n_aligned = AB_SRC_COL // LANES
    sa = np.array(list(range(n_aligned)) + [n_aligned + j for j in range(COL_AB - n_aligned)] + [n_aligned],
                  np.int32)
    sb = np.minimum(sa + 1, w_in.shape[2] // LANES).astype(np.int32)
    cls = np.array([0] * n_aligned + [1] * (COL_AB - n_aligned) + [2], np.int32)
    rows = w_in.shape[1]
    grid_spec = pltpu.PrefetchScalarGridSpec(
        num_scalar_prefetch=3,
        grid=(NW // LANES,),
        in_specs=[pl.BlockSpec((None, rows, LANES), lambda j, sa, sb, cls: (layer, 0, sa[j])),
                  pl.BlockSpec((None, rows, LANES), lambda j, sa, sb, cls: (layer, 0, sb[j]))],
        out_specs=pl.BlockSpec((rows, LANES), lambda j, sa, sb, cls: (0, j)),
    )
    return pl.pallas_call(
        _repack_kernel,
        grid_spec=grid_spec,
        out_shape=jax.ShapeDtypeStruct((rows, NW), BF16),
        compiler_params=_cparams(("arbitrary",)),
        name="repack_w_in",
    )(jnp.asarray(sa), jnp.asarray(sb), jnp.asarray(cls), w_in, w_in)


def kernel(x, w_in, sgu_norm_g, sgu_norm_b, sgu_w, sgu_b, dn_conv_w, dn_a_log, dn_dt_bias, dn_norm_g,
           diff_lambda_q1, diff_lambda_k1, diff_lambda_q2, diff_lambda_k2, diff_norm_g, w_out, ln1_g, ln1_b,
           router_w, router_bias, moe_w_gate, moe_w_up, moe_w_down, ln2_g, ln2_b):
    B, S, D = x.shape
    T = B * S
    xf = x.reshape(T, D)
    xb = xf.astype(BF16)
    cos_t, sin_t = _rope_tables(S)
    rw_pad = jnp.zeros((D, LANES), F32).at[:, :N_EXPERTS].set(router_w)
    rb_pad = jnp.zeros((1, LANES), F32).at[0, :N_EXPERTS].set(router_bias)
    for l in range(DEPTH):
        h = in_proj(xb, repack_w_in(w_in, l))
        y_a = sgu(h, sgu_norm_g[l], sgu_norm_b[l], sgu_w[l], sgu_b[l])
        y_b = deltanet(h, dn_gates(h, dn_a_log[l], dn_dt_bias[l]), dn_conv_w[l], dn_norm_g[l], B, S)
        qk = rope(h, cos_t, sin_t)
        lam_params = jnp.stack([diff_lambda_q1[l], diff_lambda_k1[l], diff_lambda_q2[l], diff_lambda_k2[l]])
        lambda_init = 0.8 - 0.6 * math.exp(-0.3 * l)
        y_c = diff_attention(qk, vprep(h, B, S), lam_params, diff_norm_g[l], B, S, lambda_init)
        x1, xp, rt = out_proj_ln_router(y_a, y_b, y_c, xf, w_out[l], ln1_g[l], ln1_b[l], rw_pad, rb_pad)
        eidx = rt[:, :TOP_K].astype(jnp.int32)
        y = moe(xp, eidx, moe_w_gate, moe_w_up, moe_w_down, l)
        xf, xb = combine_ln(x1, y, rt, ln2_g[l], ln2_b[l])
    return xf.reshape(B, S, D)
```

```python
import functools
import math

import numpy as np
import jax
import jax.numpy as jnp
from jax import lax
from jax.experimental import pallas as pl
from jax.experimental.pallas import tpu as pltpu
from jax.experimental.pallas import tpu_sc as plsc

F32 = jnp.float32
BF16 = jnp.bfloat16
HIGHEST = lax.Precision.HIGHEST

D_MODEL = 2048
DEPTH = 2
CHUNK = 64
HEAD_DIM = 128
SGU_GROUPS = 4
SGU_WIDTH = SGU_GROUPS * HEAD_DIM
SGU_BLOCK = 128
DN_HEADS = 6
DN_CONV = 4
DIFF_HEADS = 6
DIFF_QK_DIM = 64
ROPE_THETA = 10000.0
N_EXPERTS = 32
N_EXPERT_GROUPS = 4
EXPERTS_PER_GROUP = N_EXPERTS // N_EXPERT_GROUPS
TOP_K = 2
D_EXPERT = D_MODEL // 2
DEEPNORM_ALPHA = (2 * DEPTH) ** 0.25
LN_EPS = 1e-5
RMS_EPS = 1e-6

LANES = 128
VMEM_LIMIT = 56 * 1024 * 1024

HW = DN_HEADS * HEAD_DIM
COL_AU, COL_AV = 0, 4
COL_BQ, COL_BK, COL_BV, COL_BG = 8, 14, 20, 26
COL_CQ, COL_CK, COL_CV = 32, 38, 44
COL_AB = 50
AB_SRC_COL = 2 * SGU_WIDTH + 4 * HW
NW = 51 * LANES
N_TILES_IN = 3

MOE_BLK = 256


def _cparams(sem, vmem=VMEM_LIMIT):
    return pltpu.CompilerParams(dimension_semantics=sem, vmem_limit_bytes=vmem)


HALF_D = D_MODEL // 2


def _pack_rows(x):
    lo = lax.bitcast_convert_type(x[:, :HALF_D].astype(BF16).astype(F32), jnp.uint32)
    hi = lax.bitcast_convert_type(x[:, HALF_D:].astype(BF16).astype(F32), jnp.uint32)
    return (hi & jnp.uint32(0xFFFF0000)) | (lo >> 16)


def _unpack_rows(w):
    lo = lax.bitcast_convert_type(w << 16, F32)
    hi = lax.bitcast_convert_type(w & jnp.uint32(0xFFFF0000), F32)
    return jnp.concatenate([lo, hi], axis=1)


N_PLANES = 4
PLANE_W = HALF_D // N_PLANES


def _store_planes(ref, packed):
    for j in range(N_PLANES):
        ref[j] = packed[:, j * PLANE_W:(j + 1) * PLANE_W]


def _load_planes(ref):
    return jnp.concatenate([ref[j] for j in range(N_PLANES)], axis=1)


SC_WINDOW = 128


def sc_gather(x, idx):
    M = idx.shape[0]
    W = x.shape[1]
    mesh = plsc.VectorSubcoreMesh(core_axis_name="core", subcore_axis_name="subcore")

    @functools.partial(pl.kernel, out_type=jax.ShapeDtypeStruct((M, W), x.dtype), mesh=mesh, scratch_types=[])
    def gather_kernel(x_hbm, i_hbm, o_hbm):
        def body(i_vmem, o_vmem):
            pltpu.sync_copy(x_hbm.at[i_vmem.at[0]], o_vmem)

        pltpu.emit_pipeline(
            body,
            grid=(M // SC_WINDOW,),
            in_specs=[pl.BlockSpec((1, SC_WINDOW), index_map=lambda i: (0, i))],
            out_specs=[pl.BlockSpec((SC_WINDOW, W), index_map=lambda i: (i, 0))],
            core_axis_name=("core", "subcore"),
            dimension_semantics=(pltpu.PARALLEL,),
        )(i_hbm, o_hbm)

    return gather_kernel(x, idx.reshape(1, M))


def _gather_planes(x, idx):
    n = x.shape[1]
    flat_idx = (jnp.arange(N_PLANES, dtype=jnp.int32)[:, None] * n + idx[None, :]).reshape(-1)
    return sc_gather(x.reshape(N_PLANES * n, PLANE_W), flat_idx).reshape(N_PLANES, idx.shape[0], PLANE_W)


def _matmul_kernel(x_ref, w_ref, o_ref):
    o_ref[...] = jnp.dot(x_ref[...], w_ref[...], preferred_element_type=F32).astype(o_ref.dtype)


def in_proj(xb, w):
    T = xb.shape[0]
    tm = min(512, T)
    tn = NW // N_TILES_IN
    return pl.pallas_call(
        _matmul_kernel,
        grid=(N_TILES_IN, T // tm),
        in_specs=[pl.BlockSpec((tm, D_MODEL), lambda j, i: (i, 0)),
                  pl.BlockSpec((D_MODEL, tn), lambda j, i: (0, j))],
        out_specs=pl.BlockSpec((tm, tn), lambda j, i: (i, j)),
        out_shape=jax.ShapeDtypeStruct((T, NW), BF16),
        compiler_params=_cparams(("arbitrary", "arbitrary")),
        name="in_proj",
    )(xb, w)


def _gelu(t):
    return t * (lax.erf(t * (2.0 ** -0.5)) + 1.0) * 0.5


def _sgu_kernel(u_ref, v_ref, ng_ref, nb_ref, w_ref, b_ref, o_ref, *, rows):
    u = _gelu(u_ref[...].astype(F32))
    v = _gelu(v_ref[...].astype(F32))
    mu = jnp.mean(v, axis=-1, keepdims=True)
    var = jnp.mean(jnp.square(v - mu), axis=-1, keepdims=True)
    vn = ((v - mu) * lax.rsqrt(var + LN_EPS) * ng_ref[...] + nb_ref[...]).astype(BF16)
    ii = lax.broadcasted_iota(jnp.int32, (SGU_BLOCK, SGU_BLOCK), 0)
    jj = lax.broadcasted_iota(jnp.int32, (SGU_BLOCK, SGU_BLOCK), 1)
    w = jnp.where(jj // CHUNK <= ii // CHUNK, w_ref[...], 0.0).astype(BF16)
    bias = b_ref[...]
    for n in range(rows // SGU_BLOCK):
        sl = slice(n * SGU_BLOCK, (n + 1) * SGU_BLOCK)
        s = jnp.dot(w, vn[sl], preferred_element_type=F32) + bias
        o_ref[sl, :] = (u[sl] * s).astype(o_ref.dtype)


def sgu(h, norm_g, norm_b, w_s, b_s):
    T = h.shape[0]
    rows = min(1024, T)
    return pl.pallas_call(
        functools.partial(_sgu_kernel, rows=rows),
        grid=(T // rows, SGU_GROUPS),
        in_specs=[pl.BlockSpec((rows, LANES), lambda i, g: (i, COL_AU + g)),
                  pl.BlockSpec((rows, LANES), lambda i, g: (i, COL_AV + g)),
                  pl.BlockSpec((1, LANES), lambda i, g: (0, g)),
                  pl.BlockSpec((1, LANES), lambda i, g: (0, g)),
                  pl.BlockSpec((None, SGU_BLOCK, SGU_BLOCK), lambda i, g: (g, 0, 0)),
                  pl.BlockSpec((None, SGU_BLOCK, 1), lambda i, g: (g, 0, 0))],
        out_specs=pl.BlockSpec((rows, LANES), lambda i, g: (i, g)),
        out_shape=jax.ShapeDtypeStruct((T, SGU_WIDTH), BF16),
        compiler_params=_cparams(("arbitrary", "arbitrary")),
        name="sgu",
    )(h, h, norm_g.reshape(1, SGU_WIDTH), norm_b.reshape(1, SGU_WIDTH), w_s,
      b_s.reshape(SGU_GROUPS, SGU_BLOCK, 1))


def _rope_kernel(x_ref, cos_ref, sin_ref, o_ref):
    half = DIFF_QK_DIM // 2
    scale = jnp.where(pl.program_id(1) < DIFF_HEADS // ROPE_HEADS, DIFF_QK_DIM ** -0.5 * math.log2(math.e), 1.0)
    cos = cos_ref[...]
    sin = sin_ref[...]
    lane = lax.broadcasted_iota(jnp.int32, cos.shape, 1)
    for hh in range(ROPE_HEADS):
        hl = slice(hh * LANES, (hh + 1) * LANES)
        x = x_ref[:, hl].astype(F32)
        partner = jnp.where(lane % DIFF_QK_DIM < half,
                            pltpu.roll(x, LANES - half, axis=1), pltpu.roll(x, half, axis=1))
        o_ref[:, hl] = ((x * cos + partner * sin) * scale).astype(o_ref.dtype)


def _rope_tables(S):
    half = DIFF_QK_DIM // 2
    inv_freq = ROPE_THETA ** (-jnp.arange(half, dtype=F32) / half)
    ang = jnp.arange(S, dtype=F32)[:, None] * inv_freq[None, :]
    cos, sin = jnp.cos(ang), jnp.sin(ang)
    cos_t = jnp.tile(cos, (1, LANES // half))
    sin_t = jnp.tile(jnp.concatenate([-sin, sin], axis=1), (1, LANES // DIFF_QK_DIM))
    return cos_t, sin_t


ROPE_HEADS = 2
ROPE_ROWS = 1024


def rope(h, cos_t, sin_t):
    T = h.shape[0]
    S = cos_t.shape[0]
    rows = min(ROPE_ROWS, S)
    ns = S // rows
    width = ROPE_HEADS * LANES
    return pl.pallas_call(
        _rope_kernel,
        grid=(T // rows, 2 * DIFF_HEADS // ROPE_HEADS),
        in_specs=[pl.BlockSpec((rows, width), lambda i, j: (i, COL_CQ // ROPE_HEADS + j)),
                  pl.BlockSpec((rows, LANES), lambda i, j: (i % ns, 0)),
                  pl.BlockSpec((rows, LANES), lambda i, j: (i % ns, 0))],
        out_specs=pl.BlockSpec((rows, width), lambda i, j: (i, j)),
        out_shape=jax.ShapeDtypeStruct((T, 2 * HW), BF16),
        compiler_params=_cparams(("arbitrary", "arbitrary")),
        name="rope",
    )(h, cos_t, sin_t)


ATT_TILE = 1024
VT_ROWS = HEAD_DIM + 16


def _vprep_kernel(v_ref, o_ref):
    vt = v_ref[...].astype(F32).T
    o_ref[0:HEAD_DIM, :] = vt.astype(o_ref.dtype)
    row = lax.broadcasted_iota(jnp.int32, (VT_ROWS - HEAD_DIM, vt.shape[1]), 0)
    o_ref[HEAD_DIM:VT_ROWS, :] = jnp.where(row == 0, 1.0, 0.0).astype(o_ref.dtype)


def vprep(h, B, S):
    tk = min(ATT_TILE, S)
    nk = S // tk
    return pl.pallas_call(
        _vprep_kernel,
        grid=(B, DIFF_HEADS, nk),
        in_specs=[pl.BlockSpec((tk, LANES), lambda b, hd, j: (b * nk + j, COL_CV + hd))],
        out_specs=pl.BlockSpec((None, None, VT_ROWS, tk), lambda b, hd, j: (b * DIFF_HEADS + hd, j, 0, 0)),
        out_shape=jax.ShapeDtypeStruct((B * DIFF_HEADS, nk, VT_ROWS, tk), BF16),
        compiler_params=_cparams(("arbitrary", "arbitrary", "arbitrary")),
        name="vprep",
    )(h)


def _attn_kernel(q_ref, k_ref, vt_ref, lam_ref, g_ref, o_ref, acc_ref, *, tq, lambda_init):
    i = pl.program_id(2)
    q = q_ref[...]
    lane = lax.broadcasted_iota(jnp.int32, q.shape, 1)
    zero = jnp.zeros_like(q)
    qs = (jnp.where(lane < DIFF_QK_DIM, q, zero), jnp.where(lane >= DIFF_QK_DIM, q, zero))
    acc_ref[...] = jnp.zeros(acc_ref.shape, F32)
    neg = jnp.full((1, tq), -jnp.inf, F32)

    def block(j, ms, masked):
        kb = k_ref[pl.ds(pl.multiple_of(j * tq, tq), tq), :]
        vb = vt_ref[j]
        s = [lax.dot_general(kb, qs[t], (((1,), (1,)), ((), ())), preferred_element_type=F32)
             for t in range(2)]
        if masked:
            kr = lax.broadcasted_iota(jnp.int32, (tq, tq), 0)
            qc = lax.broadcasted_iota(jnp.int32, (tq, tq), 1)
            allowed = kr // CHUNK <= qc // CHUNK
        new = []
        for t in range(2):
            st = jnp.where(allowed, s[t], -jnp.inf) if masked else s[t]
            m_new = jnp.maximum(ms[t], jnp.max(st, axis=0, keepdims=True))
            alpha = jnp.exp2(ms[t] - m_new)
            p = jnp.exp2(st - m_new).astype(BF16)
            acc_ref[t] = alpha * acc_ref[t] + jnp.dot(vb, p, preferred_element_type=F32)
            new.append(m_new)
        return tuple(new)

    ms = lax.fori_loop(0, i, lambda j, ms: block(j, ms, False), (neg, neg))
    block(i, ms, True)

    lam_v = lam_ref[...]
    lam = (jnp.exp(jnp.sum(lam_v[0:1] * lam_v[1:2])) - jnp.exp(jnp.sum(lam_v[2:3] * lam_v[3:4]))
           + lambda_init)
    a0 = acc_ref[0]
    a1 = acc_ref[1]
    o = a0[:HEAD_DIM] / a0[HEAD_DIM:HEAD_DIM + 1] - lam * (a1[:HEAD_DIM] / a1[HEAD_DIM:HEAD_DIM + 1])
    y = o * lax.rsqrt(jnp.mean(o * o, axis=0, keepdims=True) + RMS_EPS) * g_ref[...]
    o_ref[...] = (y * (1.0 - lambda_init)).T.astype(o_ref.dtype)


def diff_attention(qk, vt, lam_params, norm_g, B, S, lambda_init):
    T = qk.shape[0]
    tq = min(ATT_TILE, S)
    nq = S // tq
    return pl.pallas_call(
        functools.partial(_attn_kernel, tq=tq, lambda_init=lambda_init),
        grid=(B, DIFF_HEADS, nq),
        in_specs=[pl.BlockSpec((tq, LANES), lambda b, hd, i: (b * nq + i, hd)),
                  pl.BlockSpec((S, LANES), lambda b, hd, i: (b, DIFF_HEADS + hd)),
                  pl.BlockSpec((None, nq, VT_ROWS, tq), lambda b, hd, i: (b * DIFF_HEADS + hd, 0, 0, 0)),
                  pl.BlockSpec((4, DIFF_QK_DIM), lambda b, hd, i: (0, 0)),
                  pl.BlockSpec((HEAD_DIM, 1), lambda b, hd, i: (0, 0))],
        out_specs=pl.BlockSpec((tq, LANES), lambda b, hd, i: (b * nq + i, hd)),
        out_shape=jax.ShapeDtypeStruct((T, HW), BF16),
        scratch_shapes=[pltpu.VMEM((2, VT_ROWS, tq), F32)],
        compiler_params=_cparams(("arbitrary", "arbitrary", "arbitrary")),
        name="diff_attn",
    )(qk, qk, vt, lam_params, norm_g.reshape(HEAD_DIM, 1))


def _silu(t):
    return t * jax.nn.sigmoid(t)


def _softplus(t):
    return jnp.maximum(t, 0.0) + jnp.log1p(jnp.exp(-jnp.abs(t)))


def _dot32(a, b):
    return jnp.dot(a, b, preferred_element_type=F32, precision=HIGHEST)


def _mm(a, b):
    return jnp.dot(a.astype(BF16), b.astype(BF16), preferred_element_type=F32)


def _mm_nt(a, b):
    return lax.dot_general(a.astype(BF16), b.astype(BF16), (((1,), (1,)), ((), ())),
                           preferred_element_type=F32)


DN_ROWS = 1024
DN_HB = 2


def _gates_kernel(ab_ref, alog_ref, dt_ref, o_ref, *, rows):
    ab = ab_ref[...].astype(F32)
    g_all = -jnp.exp(alog_ref[...]) * _softplus(ab + dt_ref[...])
    beta_all = jax.nn.sigmoid(ab)
    ri = lax.broadcasted_iota(jnp.int32, (CHUNK, CHUNK), 0)
    ci = lax.broadcasted_iota(jnp.int32, (CHUNK, CHUNK), 1)
    tri = (ri >= ci).astype(F32)
    is_decay_lane = lax.broadcasted_iota(jnp.int32, (CHUNK, LANES), 1) < DN_HEADS
    for c in range(rows // CHUNK):
        sl = slice(c * CHUNK, (c + 1) * CHUNK)
        o_ref[sl, :] = jnp.where(is_decay_lane, _dot32(tri, g_all[sl]), beta_all[sl])


def dn_gates(h, a_log, dt_bias):
    T = h.shape[0]
    rows = min(512, T)
    alog_row = jnp.zeros((1, LANES), F32).at[0, :DN_HEADS].set(a_log)
    dt_row = jnp.zeros((1, LANES), F32).at[0, :DN_HEADS].set(dt_bias)
    row = pl.BlockSpec((1, LANES), lambda i: (0, 0))
    return pl.pallas_call(
        functools.partial(_gates_kernel, rows=rows),
        grid=(T // rows,),
        in_specs=[pl.BlockSpec((rows, LANES), lambda i: (i, COL_AB)), row, row],
        out_specs=pl.BlockSpec((rows, LANES), lambda i: (i, 0)),
        out_shape=jax.ShapeDtypeStruct((T, LANES), F32),
        compiler_params=_cparams(("arbitrary",)),
        name="dn_gates",
    )(h, alog_row, dt_row)


def _dn_kernel(q_ref, k_ref, v_ref, gate_ref, gb_ref, cwq_ref, cwk_ref, cwv_ref, ng_ref, o_ref,
               ext_ref, tail_ref, state_ref, *, rows, hb):
    hd0 = pl.program_id(1) * hb
    r = pl.program_id(2)
    pad = 8
    nc = rows // CHUNK

    @pl.when(r == 0)
    def _():
        tail_ref[...] = jnp.zeros(tail_ref.shape, F32)
        state_ref[...] = jnp.zeros(state_ref.shape, F32)

    def conv_silu(x_ref, cw_ref, slot):
        x = x_ref[...].astype(F32)
        ext_ref[0:pad, :] = tail_ref[slot]
        ext_ref[pad:pad + rows, :] = x
        tail_ref[slot] = x[rows - pad:rows]
        cw = cw_ref[...]
        acc = None
        for t in range(DN_CONV):
            off = pad - (DN_CONV - 1) + t
            term = ext_ref[off:off + rows, :] * cw[t:t + 1, :]
            acc = term if acc is None else acc + term
        return _silu(acc)

    def l2n(t):
        return t * lax.rsqrt(jnp.sum(t * t, axis=-1, keepdims=True) + RMS_EPS)

    qf = conv_silu(q_ref, cwq_ref, 0)
    kf = conv_silu(k_ref, cwk_ref, 1)
    vf = conv_silu(v_ref, cwv_ref, 2)
    gb = gb_ref[...]
    lane = lax.broadcasted_iota(jnp.int32, gb.shape, 1)

    ri = lax.broadcasted_iota(jnp.int32, (CHUNK, CHUNK), 0)
    ci = lax.broadcasted_iota(jnp.int32, (CHUNK, CHUNK), 1)
    causal = ri >= ci
    strict = ri > ci
    eye = (ri == ci).astype(F32)

    units = []
    for hh in range(hb):
        hl = slice(hh * HEAD_DIM, (hh + 1) * HEAD_DIM)
        q = l2n(qf[:, hl]) * (HEAD_DIM ** -0.5)
        k = l2n(kf[:, hl])
        gc_col = jnp.sum(jnp.where(lane == hd0 + hh, gb, 0.0), axis=-1, keepdims=True)
        beta = jnp.sum(jnp.where(lane == hd0 + hh + DN_HEADS, gb, 0.0), axis=-1, keepdims=True)
        gcb = jnp.broadcast_to(gc_col, (rows, HEAD_DIM))
        eg = jnp.exp(gcb)
        kb = k * beta
        rhs = jnp.concatenate([vf[:, hl] * beta, kb * eg], axis=-1)
        qg = q * eg
        for c in range(nc):
            sl = slice(c * CHUNK, (c + 1) * CHUNK)
            units.append(dict(q=q[sl], k=k[sl], kb=kb[sl], rhs=rhs[sl], qg=qg[sl], g=gcb[sl]))
    for un in units:
        g = un["g"]
        gdiff = g[:, :CHUNK] - g.T[:CHUNK, :]
        un["decay"] = jnp.exp(jnp.where(causal, gdiff, -jnp.inf))
    for un in units:
        un["m"] = jnp.where(strict, _mm_nt(un["kb"], un["k"]) * un["decay"], 0.0)
    for un in units:
        un["tinv"] = eye - un["m"]
        un["pw"] = _mm(un["m"], un["m"])
    for it in range(5):
        for un in units:
            un["tinv"] = un["tinv"] + _mm(un["tinv"], un["pw"])
        if it < 4:
            for un in units:
                un["pw"] = _mm(un["pw"], un["pw"])
    for un in units:
        un["sol"] = _mm(un["tinv"], un["rhs"])
        un["attn"] = _mm_nt(un["q"], un["k"]) * un["decay"]

    gate = gate_ref[...].astype(F32)
    ng = ng_ref[...]
    states = [state_ref[hh] for hh in range(hb)]
    for c in range(nc):
        sl = slice(c * CHUNK, (c + 1) * CHUNK)
        for hh in range(hb):
            un = units[hh * nc + c]
            hl = slice(hh * HEAD_DIM, (hh + 1) * HEAD_DIM)
            u, w = un["sol"][:, :HEAD_DIM], un["sol"][:, HEAD_DIM:]
            g = un["g"]
            g_last = g[CHUNK - 1:CHUNK, :]
            sb = states[hh].astype(BF16)
            v_new = u - jnp.dot(w.astype(BF16), sb, preferred_element_type=F32)
            vnb = v_new.astype(BF16)
            o = (jnp.dot(un["qg"].astype(BF16), sb, preferred_element_type=F32)
                 + jnp.dot(un["attn"].astype(BF16), vnb, preferred_element_type=F32))
            k_dec = un["k"] * jnp.exp(g_last - g)
            states[hh] = states[hh] * jnp.exp(g_last) + jnp.dot(k_dec.T.astype(BF16), vnb,
                                                                preferred_element_type=F32)
            y = o * lax.rsqrt(jnp.mean(o * o, axis=-1, keepdims=True) + RMS_EPS) * ng
            o_ref[sl, hl] = (y * _silu(gate[sl, hl])).astype(o_ref.dtype)
    for hh in range(hb):
        state_ref[hh] = states[hh]


def deltanet(h, gb, conv_w, norm_g, B, S):
    T = h.shape[0]
    rows = min(DN_ROWS, S)
    nr = S // rows
    hb = DN_HB
    width = hb * HEAD_DIM

    def col(c):
        return pl.BlockSpec((rows, width), lambda b, hg, r: (b * nr + r, c // hb + hg))

    def cw(c):
        return pl.BlockSpec((DN_CONV, width), lambda b, hg, r: (0, c // hb + hg))

    return pl.pallas_call(
        functools.partial(_dn_kernel, rows=rows, hb=hb),
        grid=(B, DN_HEADS // hb, nr),
        in_specs=[col(COL_BQ), col(COL_BK), col(COL_BV), col(COL_BG),
                  pl.BlockSpec((rows, LANES), lambda b, hg, r: (b * nr + r, 0)),
                  cw(0), cw(DN_HEADS), cw(2 * DN_HEADS),
                  pl.BlockSpec((1, LANES), lambda b, hg, r: (0, 0))],
        out_specs=pl.BlockSpec((rows, width), lambda b, hg, r: (b * nr + r, hg)),
        out_shape=jax.ShapeDtypeStruct((T, HW), BF16),
        scratch_shapes=[pltpu.VMEM((rows + 8, width), F32), pltpu.VMEM((3, 8, width), F32),
                        pltpu.VMEM((hb, HEAD_DIM, HEAD_DIM), F32)],
        compiler_params=_cparams(("arbitrary", "arbitrary", "arbitrary")),
        name="deltanet",
    )(h, h, h, h, gb, conv_w, conv_w, conv_w, norm_g.reshape(1, LANES))


def _layer_norm(z, g, b):
    mu = jnp.mean(z, axis=-1, keepdims=True)
    var = jnp.mean(jnp.square(z - mu), axis=-1, keepdims=True)
    return (z - mu) * lax.rsqrt(var + LN_EPS) * g + b


def _outproj_kernel(ya_ref, yb_ref, yc_ref, x_ref, wa_ref, wb_ref, wc_ref, g_ref, b_ref, rwh_ref, rwl_ref,
                    rb_ref, x1_ref, xp_ref, rt_ref):
    mixed = (jnp.dot(ya_ref[...], wa_ref[...], preferred_element_type=F32)
             + jnp.dot(yb_ref[...], wb_ref[...], preferred_element_type=F32)
             + jnp.dot(yc_ref[...], wc_ref[...], preferred_element_type=F32))
    x1 = _layer_norm(DEEPNORM_ALPHA * x_ref[...] + mixed, g_ref[...], b_ref[...])
    x1_ref[...] = x1
    _store_planes(xp_ref, _pack_rows(x1))

    x_hi = x1.astype(BF16)
    x_lo = (x1 - x_hi.astype(F32)).astype(BF16)
    logits = (jnp.dot(x_hi, rwh_ref[...], preferred_element_type=F32)
              + jnp.dot(x_lo, rwh_ref[...], preferred_element_type=F32)
              + jnp.dot(x_hi, rwl_ref[...], preferred_element_type=F32))
    scores = jax.nn.sigmoid(logits)
    tm = scores.shape[0]
    lane = lax.broadcasted_iota(jnp.int32, scores.shape, 1)
    lanef = lane.astype(F32)
    neg = -jnp.inf
    big = float(LANES)
    sel = jnp.where(lane < N_EXPERTS, scores + rb_ref[...], neg)

    def top2(mg):
        t1 = jnp.max(mg, axis=-1, keepdims=True)
        i1 = jnp.min(jnp.where(mg == t1, lanef, big), axis=-1, keepdims=True)
        mg2 = jnp.where(lanef == i1, neg, mg)
        t2 = jnp.max(mg2, axis=-1, keepdims=True)
        i2 = jnp.min(jnp.where(mg2 == t2, lanef, big), axis=-1, keepdims=True)
        return t1 + t2, i1, i2

    best_s = best_1 = best_2 = None
    for gidx in range(N_EXPERT_GROUPS):
        in_g = (lane >= gidx * EXPERTS_PER_GROUP) & (lane < (gidx + 1) * EXPERTS_PER_GROUP)
        gs, i1, i2 = top2(jnp.where(in_g, sel, neg))
        if best_s is None:
            best_s, best_1, best_2 = gs, i1, i2
        else:
            take = gs > best_s
            best_s = jnp.where(take, gs, best_s)
            best_1 = jnp.where(take, i1, best_1)
            best_2 = jnp.where(take, i2, best_2)
    s1 = jnp.sum(jnp.where(lanef == best_1, scores, 0.0), axis=-1, keepdims=True)
    s2 = jnp.sum(jnp.where(lanef == best_2, scores, 0.0), axis=-1, keepdims=True)
    tot = s1 + s2
    l8 = lax.broadcasted_iota(jnp.int32, (tm, 8), 1)
    rt_ref[...] = jnp.where(l8 == 0, best_1, jnp.where(l8 == 1, best_2, jnp.where(
        l8 == 2, s1 / tot, jnp.where(l8 == 3, s2 / tot, 0.0))))


def out_proj_ln_router(ya, yb, yc, x, w_out, ln_g, ln_b, rw_pad, rb_pad):
    T = x.shape[0]
    tm = min(256, T)
    wa = w_out[:SGU_WIDTH].astype(BF16)
    wb = w_out[SGU_WIDTH:SGU_WIDTH + HW].astype(BF16)
    wc = w_out[SGU_WIDTH + HW:].astype(BF16)
    rw_hi = rw_pad.astype(BF16)
    rw_lo = (rw_pad - rw_hi.astype(F32)).astype(BF16)

    def rowblk(width):
        return pl.BlockSpec((tm, width), lambda i: (i, 0))

    def full(shape):
        return pl.BlockSpec(shape, lambda i: (0, 0))

    return pl.pallas_call(
        _outproj_kernel,
        grid=(T // tm,),
        in_specs=[rowblk(SGU_WIDTH), rowblk(HW), rowblk(HW), rowblk(D_MODEL),
                  full((SGU_WIDTH, D_MODEL)), full((HW, D_MODEL)), full((HW, D_MODEL)),
                  full((1, D_MODEL)), full((1, D_MODEL)), full((D_MODEL, LANES)), full((D_MODEL, LANES)),
                  full((1, LANES))],
        out_specs=[rowblk(D_MODEL), pl.BlockSpec((N_PLANES, tm, PLANE_W), lambda i: (0, i, 0)), rowblk(8)],
        out_shape=[jax.ShapeDtypeStruct((T, D_MODEL), F32),
                   jax.ShapeDtypeStruct((N_PLANES, T, PLANE_W), jnp.uint32), jax.ShapeDtypeStruct((T, 8), F32)],
        compiler_params=_cparams(("arbitrary",)),
        name="out_proj_ln_router",
    )(ya, yb, yc, x, wa, wb, wc, ln_g.reshape(1, D_MODEL), ln_b.reshape(1, D_MODEL), rw_hi, rw_lo, rb_pad)


def _route_tables(eidx, T, nb):
    A = T * TOP_K
    flat_e = eidx.reshape(A)
    order = jnp.argsort(flat_e, stable=True).astype(jnp.int32)
    experts = jnp.arange(N_EXPERTS, dtype=jnp.int32)
    counts = jnp.sum((flat_e[:, None] == experts[None, :]).astype(jnp.int32), axis=0)
    padded = (counts + MOE_BLK - 1) // MOE_BLK * MOE_BLK
    pad_end = jnp.cumsum(padded)
    pad_start = pad_end - padded
    start = jnp.cumsum(counts) - counts
    pos = jnp.arange(nb * MOE_BLK, dtype=jnp.int32)
    e_of_pos = jnp.sum((pos[:, None] >= pad_end[None, :]).astype(jnp.int32), axis=1)
    onehot = e_of_pos[:, None] == experts[None, :]

    def lookup(tab):
        return jnp.sum(jnp.where(onehot, tab[None, :], 0), axis=1)

    rank = pos - lookup(pad_start)
    valid = rank < lookup(counts)
    a_sorted = order[jnp.clip(lookup(start) + rank, 0, A - 1)]
    src = jnp.where(valid, a_sorted // TOP_K, pos % T)
    block_expert = jnp.minimum(e_of_pos.reshape(nb, MOE_BLK)[:, 0], N_EXPERTS - 1)
    n_valid = jnp.sum(valid.reshape(nb, MOE_BLK).astype(jnp.int32), axis=1)
    onehot_a = flat_e[:, None] == experts[None, :]
    rank_a = jnp.argsort(order).astype(jnp.int32) - jnp.sum(jnp.where(onehot_a, start[None, :], 0), axis=1)
    pos_a = jnp.sum(jnp.where(onehot_a, pad_start[None, :], 0), axis=1) + rank_a
    inv = pos_a.reshape(T, TOP_K).T.reshape(A)
    return block_expert, n_valid, src, inv


def _moe_kernel(be_ref, nv_ref, x_ref, wg_ref, wu_ref, wd_ref, o_ref):
    n_valid = nv_ref[pl.program_id(0)]

    @pl.when(n_valid > 0)
    def _():
        xb = _unpack_rows(_load_planes(x_ref))
        hg = jnp.dot(xb, wg_ref[...], preferred_element_type=F32)
        hu = jnp.dot(xb, wu_ref[...], preferred_element_type=F32)
        hb = _silu(hg) * hu
        _store_planes(o_ref, _pack_rows(jnp.dot(hb, wd_ref[...], preferred_element_type=F32)))

    @pl.when(n_valid == 0)
    def _():
        o_ref[...] = jnp.zeros(o_ref.shape, o_ref.dtype)


def moe(xp, eidx, wg, wu, wd, layer):
    T = xp.shape[1]
    A = T * TOP_K
    nb = -(-A // MOE_BLK) + N_EXPERTS
    block_expert, n_valid, src, inv = _route_tables(eidx, T, nb)
    xs = _gather_planes(xp, src)

    def w_blk(rows, cols, buffers):
        return pl.BlockSpec((None, None, rows, cols), lambda b, be, nv: (layer, be[b], 0, 0),
                            pipeline_mode=pl.Buffered(buffers))

    row_blk = pl.BlockSpec((N_PLANES, MOE_BLK, PLANE_W), lambda b, be, nv: (0, b, 0))
    grid_spec = pltpu.PrefetchScalarGridSpec(
        num_scalar_prefetch=2,
        grid=(nb,),
        in_specs=[row_blk, w_blk(D_MODEL, D_EXPERT, 1), w_blk(D_MODEL, D_EXPERT, 1), w_blk(D_EXPERT, D_MODEL, 2)],
        out_specs=row_blk,
    )
    ys = pl.pallas_call(
        _moe_kernel,
        grid_spec=grid_spec,
        out_shape=jax.ShapeDtypeStruct((N_PLANES, nb * MOE_BLK, PLANE_W), jnp.uint32),
        compiler_params=_cparams(("arbitrary",)),
        name="moe",
    )(block_expert, n_valid, xs, wg, wu, wd)
    return _gather_planes(ys, inv)


def _combine_kernel(x_ref, y0_ref, y1_ref, rt_ref, g_ref, b_ref, o_ref, ob_ref):
    rt = rt_ref[...]
    ffn = _unpack_rows(_load_planes(y0_ref)) * rt[:, 2:3] + _unpack_rows(_load_planes(y1_ref)) * rt[:, 3:4]
    x2 = _layer_norm(DEEPNORM_ALPHA * x_ref[...] + ffn, g_ref[...], b_ref[...])
    o_ref[...] = x2
    ob_ref[...] = x2.astype(BF16)


def combine_ln(x1, y, rt, ln_g, ln_b):
    T = x1.shape[0]
    tm = min(256, T)
    nt = T // tm
    blk = pl.BlockSpec((tm, D_MODEL), lambda i: (i, 0))
    vec = pl.BlockSpec((1, D_MODEL), lambda i: (0, 0))
    return pl.pallas_call(
        _combine_kernel,
        grid=(nt,),
        in_specs=[blk, pl.BlockSpec((N_PLANES, tm, PLANE_W), lambda i: (0, i, 0)),
                  pl.BlockSpec((N_PLANES, tm, PLANE_W), lambda i: (0, nt + i, 0)),
                  pl.BlockSpec((tm, 8), lambda i: (i, 0)), vec, vec],
        out_specs=[blk, blk],
        out_shape=[jax.ShapeDtypeStruct((T, D_MODEL), F32), jax.ShapeDtypeStruct((T, D_MODEL), BF16)],
        compiler_params=_cparams(("arbitrary",)),
        name="combine_ln",
    )(x1, y, y, rt, ln_g.reshape(1, D_MODEL), ln_b.reshape(1, D_MODEL))


def _repack_kernel(sa_ref, sb_ref, cls_ref, a_ref, b_ref, o_ref):
    cls = cls_ref[pl.program_id(0)]
    lane = lax.broadcasted_iota(jnp.int32, a_ref.shape, 1)
    shift = AB_SRC_COL % LANES + 2 * DN_HEADS

    @pl.when(cls == 0)
    def _():
        o_ref[...] = a_ref[...].astype(o_ref.dtype)

    @pl.when(cls == 1)
    def _():
        left = pltpu.roll(a_ref[...], LANES - shift, axis=1)
        right = pltpu.roll(b_ref[...], LANES - shift, axis=1)
        o_ref[...] = jnp.where(lane < LANES - shift, left, right).astype(o_ref.dtype)

    @pl.when(cls == 2)
    def _():
        o_ref[...] = jnp.where(lane < 2 * DN_HEADS, a_ref[...], 0.0).astype(o_ref.dtype)


def repack_w_in(w_in, layer):
    n_aligned = AB_SRC_COL // LANES
    sa = np.array(list(range(n_aligned)) + [n_aligned + j for j in range(COL_AB - n_aligned)] + [n_aligned],
                  np.int32)
    sb = np.minimum(sa + 1, w_in.shape[2] // LANES).astype(np.int32)
    cls = np.array([0] * n_aligned + [1] * (COL_AB - n_aligned) + [2], np.int32)
    rows = w_in.shape[1]
    grid_spec = pltpu.PrefetchScalarGridSpec(
        num_scalar_prefetch=3,
        grid=(NW // LANES,),
        in_specs=[pl.BlockSpec((None, rows, LANES), lambda j, sa, sb, cls: (layer, 0, sa[j])),
                  pl.BlockSpec((None, rows, LANES), lambda j, sa, sb, cls: (layer, 0, sb[j]))],
        out_specs=pl.BlockSpec((rows, LANES), lambda j, sa, sb, cls: (0, j)),
    )
    return pl.pallas_call(
        _repack_kernel,
        grid_spec=grid_spec,
        out_shape=jax.ShapeDtypeStruct((rows, NW), BF16),
        compiler_params=_cparams(("arbitrary",)),
        name="repack_w_in",
    )(jnp.asarray(sa), jnp.asarray(sb), jnp.asarray(cls), w_in, w_in)


def kernel(x, w_in, sgu_norm_g, sgu_norm_b, sgu_w, sgu_b, dn_conv_w, dn_a_log, dn_dt_bias, dn_norm_g,
           diff_lambda_q1, diff_lambda_k1, diff_lambda_q2, diff_lambda_k2, diff_norm_g, w_out, ln1_g, ln1_b,
           router_w, router_bias, moe_w_gate, moe_w_up, moe_w_down, ln2_g, ln2_b):
    B, S, D = x.shape
    T = B * S
    xf = x.reshape(T, D)
    xb = xf.astype(BF16)
    cos_t, sin_t = _rope_tables(S)
    rw_pad = jnp.zeros((D, LANES), F32).at[:, :N_EXPERTS].set(router_w)
    rb_pad = jnp.zeros((1, LANES), F32).at[0, :N_EXPERTS].set(router_bias)
    for l in range(DEPTH):
        h = in_proj(xb, repack_w_in(w_in, l))
        y_a = sgu(h, sgu_norm_g[l], sgu_norm_b[l], sgu_w[l], sgu_b[l])
        y_b = deltanet(h, dn_gates(h, dn_a_log[l], dn_dt_bias[l]), dn_conv_w[l], dn_norm_g[l], B, S)
        qk = rope(h, cos_t, sin_t)
        lam_params = jnp.stack([diff_lambda_q1[l], diff_lambda_k1[l], diff_lambda_q2[l], diff_lambda_k2[l]])
        lambda_init = 0.8 - 0.6 * math.exp(-0.3 * l)
        y_c = diff_attention(qk, vprep(h, B, S), lam_params, diff_norm_g[l], B, S, lambda_init)
        x1, xp, rt = out_proj_ln_router(y_a, y_b, y_c, xf, w_out[l], ln1_g[l], ln1_b[l], rw_pad, rb_pad)
        eidx = rt[:, :TOP_K].astype(jnp.int32)
        y = moe(xp, eidx, moe_w_gate, moe_w_up, moe_w_down, l)
        xf, xb = combine_ln(x1, y, rt, ln2_g[l], ln2_b[l])
    return xf.reshape(B, S, D)
```
